```python
import math
import jax, jax.numpy as jnp
from jax import lax
import numpy as np

D_MODEL = 4096
BATCH = 4
SEQ = 2048
DEPTH = 4
DEC_BATCH = 8
DEC_SEQ = 1
PAST_LEN = 8192
PAGE_SIZE = 128

N_A = DEPTH // 2
N_B = DEPTH - N_A
CHUNK = 128
SGU_GROUPS = 8
E_A = 3 * D_MODEL // 4
SGU_CH = E_A // SGU_GROUPS
B_WIDTH = 3 * D_MODEL // 4
B_HD2 = 256
B_HD = B_HD2 // 2
B_HEADS = B_WIDTH // B_HD2
MEM_LEN = 256
MEM_HEADS = 4
MEM_WIDTH = D_MODEL // 4
MEM_HD = MEM_WIDTH // MEM_HEADS
N_BUCKETS = 32
MAX_DISTANCE = 128
Q_BLOCK = 128
EPS = 1e-6
IN_A = 3 * E_A + 2 * MEM_WIDTH
IN_B = 2 * B_WIDTH + 2 * MEM_WIDTH
OUT_W = B_WIDTH + MEM_WIDTH

kernel_name = "yoco_gmlp_diffattn_memxattn_step"


def rmsnorm(x, g):
    x32 = x.astype(jnp.float32)
    y = x32 * lax.rsqrt(jnp.mean(jnp.square(x32), axis=-1, keepdims=True) + EPS)
    return (y * g.astype(jnp.float32)).astype(x.dtype)


def layernorm(x, g, b):
    x32 = x.astype(jnp.float32)
    xc = x32 - jnp.mean(x32, axis=-1, keepdims=True)
    var = jnp.mean(jnp.square(xc), axis=-1, keepdims=True)
    return (xc * lax.rsqrt(var + EPS) * g.astype(jnp.float32) + b.astype(jnp.float32)).astype(x.dtype)


def rel_bucket(dist):
    n = jnp.maximum(dist, 0)
    max_exact = N_BUCKETS // 2
    nf = jnp.maximum(n, 1).astype(jnp.float32)
    large = max_exact + (jnp.log(nf / max_exact) / math.log(MAX_DISTANCE / max_exact)
                         * (N_BUCKETS - max_exact)).astype(jnp.int32)
    large = jnp.minimum(large, N_BUCKETS - 1)
    return jnp.where(n < max_exact, n, large)


def chunk_sgu(u, v, w_s, b_s):
    bsz, L, _ = v.shape
    n = -(-L // CHUNK)
    vp = jnp.pad(v, ((0, 0), (0, n * CHUNK - L), (0, 0))).reshape(bsz, n, CHUNK, SGU_GROUPS, SGU_CH)
    causal = jnp.tril(jnp.ones((CHUNK, CHUNK), dtype=bool))
    w = jnp.where(causal[None], w_s, 0.0).astype(v.dtype)
    mixed = jnp.einsum('gts,bnsgc->bntgc', w, vp) + b_s.T.astype(v.dtype)[None, None, :, :, None]
    mixed = mixed.reshape(bsz, n * CHUNK, E_A)[:, :L]
    return u * mixed


def mem_attention(q, mk, mv):
    bsz, L = q.shape[:2]
    s = jnp.einsum('blhd,bmhd->bhlm', q.astype(jnp.float32), mk.astype(jnp.float32)) * (MEM_HD ** -0.5)
    p = jax.nn.softmax(s, axis=-1)
    o = jnp.einsum('bhlm,bmhd->blhd', p, mv.astype(jnp.float32))
    return o.reshape(bsz, L, MEM_WIDTH).astype(q.dtype)


def diff_attention(q, k, v, q_pos, k_pos, rel_bias, lam):
    bsz, Lq = q.shape[:2]
    n = -(-Lq // Q_BLOCK)
    pad = n * Q_BLOCK - Lq
    qp = jnp.pad(q, ((0, 0), (0, pad), (0, 0), (0, 0), (0, 0)))
    pp = jnp.pad(q_pos, (0, pad), mode='edge')
    qb = jnp.moveaxis(qp.reshape(bsz, n, Q_BLOCK, B_HEADS, 2, B_HD), 1, 0)
    pb = pp.reshape(n, Q_BLOCK)
    k32 = k.astype(jnp.float32)
    v32 = v.astype(jnp.float32)
    table = rel_bias.astype(jnp.float32)
    scale = B_HD ** -0.5

    def block(args):
        qblk, pos = args
        s = jnp.einsum('bqhcd,bkhcd->bhcqk', qblk.astype(jnp.float32), k32) * scale
        dist = pos[:, None] - k_pos[None, :]
        bias = jnp.moveaxis(table[rel_bucket(dist)], -1, 0)
        s = jnp.where((dist >= 0)[None, None, None], s + bias[None, :, None], -jnp.inf)
        p = jax.nn.softmax(s, axis=-1)
        w = p[:, :, 0] - lam * p[:, :, 1]
        return jnp.einsum('bhqk,bkhe->bqhe', w, v32)

    o = lax.map(block, (qb, pb))
    return jnp.moveaxis(o, 0, 1).reshape(bsz, n * Q_BLOCK, B_HEADS, B_HD2)[:, :Lq]


def run_group(x, start, mem_k, mem_v, past_k, past_v,
              norm_g, final_norm_g, w_in_a, sgu_ln_g, sgu_ln_b, sgu_w, sgu_b, w_out_a,
              kv_norm_g, w_kv, w_in_b, lambda_q1, lambda_k1, lambda_q2, lambda_k2,
              subln_g, rel_bias, w_out_b):
    bsz, L, _ = x.shape
    q_pos = start + jnp.arange(L, dtype=jnp.int32)
    chunk_vs = []
    for l in range(DEPTH):
        h = rmsnorm(x, norm_g[l])
        if l < N_A:
            p = h @ w_in_a[l]
            uv, z, qm, zm = jnp.split(p, [2 * E_A, 3 * E_A, 3 * E_A + MEM_WIDTH], axis=-1)
            uv = jax.nn.gelu(uv)
            u, v = uv[..., :E_A], uv[..., E_A:]
            v = layernorm(v, sgu_ln_g[l], sgu_ln_b[l])
            chunk_vs.append(v)
            main = chunk_sgu(u, v, sgu_w[l], sgu_b[l]) * jax.nn.silu(z)
            w_out = w_out_a[l]
        else:
            j = l - N_A
            p = h @ w_in_b[j]
            q, z, qm, zm = jnp.split(p, [B_WIDTH, 2 * B_WIDTH, 2 * B_WIDTH + MEM_WIDTH], axis=-1)
            q = q.reshape(bsz, L, B_HEADS, 2, B_HD)
            lam_init = 0.8 - 0.6 * math.exp(-0.3 * l)
            f32 = jnp.float32
            lam = (jnp.exp(jnp.sum(lambda_q1[j].astype(f32) * lambda_k1[j].astype(f32)))
                   - jnp.exp(jnp.sum(lambda_q2[j].astype(f32) * lambda_k2[j].astype(f32))) + lam_init)
            o = diff_attention(q, k_all, v_all, q_pos, k_pos, rel_bias, lam)
            o = rmsnorm(o, subln_g[j]) * (1.0 - lam_init)
            main = o.reshape(bsz, L, B_WIDTH).astype(x.dtype) * jax.nn.silu(z)
            w_out = w_out_b[j]
        mem_o = mem_attention(qm.reshape(bsz, L, MEM_HEADS, MEM_HD), mem_k[l], mem_v[l]) * jax.nn.silu(zm)
        x = x + jnp.concatenate([main, mem_o], axis=-1) @ w_out
        if l == N_A - 1:
            kv = rmsnorm(x, kv_norm_g) @ w_kv
            k_new = kv[..., :B_WIDTH].reshape(bsz, L, B_HEADS, 2, B_HD)
            v_new = kv[..., B_WIDTH:].reshape(bsz, L, B_HEADS, B_HD2)
            if past_k is None:
                k_all, v_all, k_pos = k_new, v_new, q_pos
            else:
                k_all = jnp.concatenate([past_k.astype(k_new.dtype), k_new], axis=1)
                v_all = jnp.concatenate([past_v.astype(v_new.dtype), v_new], axis=1)
                k_pos = jnp.arange(start + L, dtype=jnp.int32)
    y = rmsnorm(x, final_norm_g)
    return y, k_new, v_new, jnp.stack(chunk_vs)


def setup_inputs(seed: int = 0) -> dict:
    key = jax.random.key(seed)
    ks = jax.random.split(key, 32)
    n_pages = PAST_LEN // PAGE_SIZE
    n_used = DEC_BATCH * n_pages
    n_pool = n_used + max(1, n_used // 4)
    nrm = jax.random.normal
    f = jnp.float32
    page_table = jax.random.permutation(ks[6], n_pool)[:n_used].reshape(DEC_BATCH, n_pages).astype(jnp.int32)
    return {
        "x_prompt": nrm(ks[0], (BATCH, SEQ, D_MODEL), f),
        "x_sample": nrm(ks[1], (DEC_BATCH, DEC_SEQ, D_MODEL), f),
        "cache_k": nrm(ks[2], (n_pool, PAGE_SIZE, B_HEADS, 2, B_HD), f),
        "cache_v": nrm(ks[3], (n_pool, PAGE_SIZE, B_HEADS, B_HD2), f),
        "cache_mem_k": nrm(ks[4], (DEPTH, DEC_BATCH, MEM_LEN, MEM_HEADS, MEM_HD), f),
        "cache_mem_v": nrm(ks[5], (DEPTH, DEC_BATCH, MEM_LEN, MEM_HEADS, MEM_HD), f),
        "page_table": page_table,
        "mem_prompt": nrm(ks[7], (BATCH, MEM_LEN, D_MODEL), f),
        "norm_g": 1.0 + 0.01 * nrm(ks[8], (DEPTH, D_MODEL), f),
        "final_norm_g": 1.0 + 0.01 * nrm(ks[9], (D_MODEL,), f),
        "mem_norm_g": 1.0 + 0.01 * nrm(ks[10], (D_MODEL,), f),
        "w_mem_kv": nrm(ks[11], (DEPTH, D_MODEL, 2 * MEM_WIDTH), f) * D_MODEL ** -0.5,
        "w_in_a": nrm(ks[12], (N_A, D_MODEL, IN_A), f) * D_MODEL ** -0.5,
        "sgu_ln_g": 1.0 + 0.01 * nrm(ks[13], (N_A, E_A), f),
        "sgu_ln_b": 0.01 * nrm(ks[14], (N_A, E_A), f),
        "sgu_w": nrm(ks[15], (N_A, SGU_GROUPS, CHUNK, CHUNK), f) * CHUNK ** -0.5,
        "sgu_b": 1.0 + 0.1 * nrm(ks[16], (N_A, SGU_GROUPS, CHUNK), f),
        "w_out_a": nrm(ks[17], (N_A, OUT_W, D_MODEL), f) * OUT_W ** -0.5,
        "kv_norm_g": 1.0 + 0.01 * nrm(ks[18], (D_MODEL,), f),
        "w_kv": nrm(ks[19], (D_MODEL, 2 * B_WIDTH), f) * D_MODEL ** -0.5,
        "w_in_b": nrm(ks[20], (N_B, D_MODEL, IN_B), f) * D_MODEL ** -0.5,
        "lambda_q1": 0.1 * nrm(ks[21], (N_B, B_HD), f),
        "lambda_k1": 0.1 * nrm(ks[22], (N_B, B_HD), f),
        "lambda_q2": 0.1 * nrm(ks[23], (N_B, B_HD), f),
        "lambda_k2": 0.1 * nrm(ks[24], (N_B, B_HD), f),
        "subln_g": 1.0 + 0.01 * nrm(ks[25], (N_B, B_HD2), f),
        "rel_bias": 0.5 * nrm(ks[26], (N_BUCKETS, B_HEADS), f),
        "w_out_b": nrm(ks[27], (N_B, OUT_W, D_MODEL), f) * OUT_W ** -0.5,
    }


def reference(x_prompt, x_sample, cache_k, cache_v, cache_mem_k, cache_mem_v, page_table, mem_prompt,
              norm_g, final_norm_g, mem_norm_g, w_mem_kv, w_in_a, sgu_ln_g, sgu_ln_b, sgu_w, sgu_b,
              w_out_a, kv_norm_g, w_kv, w_in_b, lambda_q1, lambda_k1, lambda_q2, lambda_k2,
              subln_g, rel_bias, w_out_b):
    weights = (norm_g, final_norm_g, w_in_a, sgu_ln_g, sgu_ln_b, sgu_w, sgu_b, w_out_a,
               kv_norm_g, w_kv, w_in_b, lambda_q1, lambda_k1, lambda_q2, lambda_k2,
               subln_g, rel_bias, w_out_b)
    bp, m_len = mem_prompt.shape[:2]
    mkv = jnp.einsum('bmd,ldf->lbmf', rmsnorm(mem_prompt, mem_norm_g), w_mem_kv)
    mem_k_p = mkv[..., :MEM_WIDTH].reshape(DEPTH, bp, m_len, MEM_HEADS, MEM_HD)
    mem_v_p = mkv[..., MEM_WIDTH:].reshape(DEPTH, bp, m_len, MEM_HEADS, MEM_HD)
    y_prompt, k_p, v_p, _ = run_group(x_prompt, 0, mem_k_p, mem_v_p, None, None, *weights)
    bd, n_pages = page_table.shape
    past_len = n_pages * cache_k.shape[1]
    past_k = cache_k[page_table].reshape(bd, past_len, B_HEADS, 2, B_HD)
    past_v = cache_v[page_table].reshape(bd, past_len, B_HEADS, B_HD2)
    y_sample, k_s, v_s, chunk_v_s = run_group(x_sample, past_len, cache_mem_k, cache_mem_v,
                                              past_k, past_v, *weights)
    return (y_prompt, y_sample, k_p, v_p, k_s, v_s, mem_k_p, mem_v_p, chunk_v_s)
```

```python
import functools
import math

import jax
import jax.numpy as jnp
from jax import lax
from jax.experimental import pallas as pl
from jax.experimental.pallas import tpu as pltpu

F32 = jnp.float32
BF16 = jnp.bfloat16

CHUNK = 128
SGU_GROUPS = 8
B_HD2 = 256
B_HD = B_HD2 // 2
MEM_HEADS = 4
N_BUCKETS = 32
MAX_DISTANCE = 128
PAGE_SIZE = 128
EPS = 1e-6
NEG = -1e30

VMEM_BYTES_V7X = 64 * 1024 * 1024
VMEM_LIMIT = VMEM_BYTES_V7X - 8 * 1024 * 1024

NT_DIMS = (((1,), (1,)), ((), ()))


def _params(*sem):
    return pltpu.CompilerParams(dimension_semantics=sem, vmem_limit_bytes=VMEM_LIMIT)


def _rms_matmul_kernel(x_ref, g_ref, w_ref, o_ref, h_ref):
    @pl.when(pl.program_id(1) == 0)
    def _():
        x = x_ref[...]
        ms = jnp.mean(x * x, axis=-1, keepdims=True)
        h_ref[...] = (x * lax.rsqrt(ms + EPS) * g_ref[...]).astype(BF16)

    o_ref[...] = jnp.dot(h_ref[...], w_ref[...], preferred_element_type=F32).astype(o_ref.dtype)


def rms_matmul(x, g, w, *, tm, tn, name):
    M, D = x.shape
    N = w.shape[1]
    tm, tn = min(tm, M), min(tn, N)
    return pl.pallas_call(
        _rms_matmul_kernel,
        grid=(M // tm, N // tn),
        in_specs=[pl.BlockSpec((tm, D), lambda i, j: (i, 0)),
                  pl.BlockSpec((1, D), lambda i, j: (0, 0)),
                  pl.BlockSpec((D, tn), lambda i, j: (0, j))],
        out_specs=pl.BlockSpec((tm, tn), lambda i, j: (i, j)),
        out_shape=jax.ShapeDtypeStruct((M, N), F32),
        scratch_shapes=[pltpu.VMEM((tm, D), BF16)],
        compiler_params=_params("parallel", "arbitrary"),
        name=name,
    )(x, g.reshape(1, D), w)


def rms_matmul_stacked(x, g, w, *, tm, tn, name):
    M, D = x.shape
    G, _, N = w.shape
    nb = N // tn
    return pl.pallas_call(
        _rms_matmul_kernel,
        grid=(M // tm, G * nb),
        in_specs=[pl.BlockSpec((tm, D), lambda i, j: (i, 0)),
                  pl.BlockSpec((1, D), lambda i, j: (0, 0)),
                  pl.BlockSpec((None, D, tn), lambda i, j: (j // nb, 0, j % nb))],
        out_specs=pl.BlockSpec((None, tm, tn), lambda i, j: (j, i, 0)),
        out_shape=jax.ShapeDtypeStruct((G * nb, M, tn), F32),
        scratch_shapes=[pltpu.VMEM((tm, D), BF16)],
        compiler_params=_params("parallel", "arbitrary"),
        name=name,
    )(x, g.reshape(1, D), w)


def _out_proj_kernel(a_ref, m_ref, w_ref, x_ref, o_ref):
    ka = a_ref.shape[1]
    acc = jnp.dot(a_ref[...], w_ref[:ka, :], preferred_element_type=F32)
    acc = acc + jnp.dot(m_ref[...], w_ref[ka:, :], preferred_element_type=F32)
    o_ref[...] = x_ref[...] + acc


def out_proj(a, m, w, x, *, tm, tn, name):
    M, ka = a.shape
    km = m.shape[1]
    N = w.shape[1]
    tm, tn = min(tm, M), min(tn, N)
    return pl.pallas_call(
        _out_proj_kernel,
        grid=(M // tm, N // tn),
        in_specs=[pl.BlockSpec((tm, ka), lambda i, j: (i, 0)),
                  pl.BlockSpec((tm, km), lambda i, j: (i, 0)),
                  pl.BlockSpec((ka + km, tn), lambda i, j: (0, j)),
                  pl.BlockSpec((tm, tn), lambda i, j: (i, j))],
        out_specs=pl.BlockSpec((tm, tn), lambda i, j: (i, j)),
        out_shape=jax.ShapeDtypeStruct((M, N), F32),
        compiler_params=_params("parallel", "arbitrary"),
        name=name,
    )(a, m, w, x)


def _rmsnorm_kernel(x_ref, g_ref, o_ref):
    x = x_ref[...]
    ms = jnp.mean(x * x, axis=-1, keepdims=True)
    o_ref[...] = x * lax.rsqrt(ms + EPS) * g_ref[...]


def rmsnorm_rows(x, g, *, tm, name):
    M, D = x.shape
    tm = min(tm, M)
    return pl.pallas_call(
        _rmsnorm_kernel,
        grid=(M // tm,),
        in_specs=[pl.BlockSpec((tm, D), lambda i: (i, 0)),
                  pl.BlockSpec((1, D), lambda i: (0, 0))],
        out_specs=pl.BlockSpec((tm, D), lambda i: (i, 0)),
        out_shape=jax.ShapeDtypeStruct((M, D), F32),
        compiler_params=_params("parallel"),
        name=name,
    )(x, g.reshape(1, D))


def _layernorm(v, g, b):
    mu = jnp.mean(v, axis=-1, keepdims=True)
    vc = v - mu
    var = jnp.mean(vc * vc, axis=-1, keepdims=True)
    return vc * lax.rsqrt(var + EPS) * g + b


def _sgu_kernel(u_ref, v_ref, z_ref, lng_ref, lnb_ref, w_ref, bt_ref, o_ref):
    rows, width = u_ref.shape
    ch = width // SGU_GROUPS
    r = lax.broadcasted_iota(jnp.int32, (CHUNK, CHUNK), 0)
    c = lax.broadcasted_iota(jnp.int32, (CHUNK, CHUNK), 1)
    causal = c <= r
    ws = [jnp.where(causal, w_ref[g], 0.0).astype(BF16) for g in range(SGU_GROUPS)]
    for n in range(rows // CHUNK):
        rs = slice(n * CHUNK, (n + 1) * CHUNK)
        vn = _layernorm(jax.nn.gelu(v_ref[rs, :]), lng_ref[...], lnb_ref[...]).astype(BF16)
        for g in range(SGU_GROUPS):
            cs = slice(g * ch, (g + 1) * ch)
            mixed = jnp.dot(ws[g], vn[:, cs], preferred_element_type=F32) + bt_ref[:, g:g + 1]
            u = jax.nn.gelu(u_ref[rs, cs])
            o_ref[rs, cs] = (u * mixed * jax.nn.silu(z_ref[rs, cs])).astype(o_ref.dtype)


def sgu_prompt(p, ln_g, ln_b, w_s, b_s, *, width, rows, name):
    M = p.shape[0]
    return pl.pallas_call(
        _sgu_kernel,
        grid=(M // rows,),
        in_specs=[pl.BlockSpec((rows, width), lambda i: (i, 0)),
                  pl.BlockSpec((rows, width), lambda i: (i, 1)),
                  pl.BlockSpec((rows, width), lambda i: (i, 2)),
                  pl.BlockSpec((1, width), lambda i: (0, 0)),
                  pl.BlockSpec((1, width), lambda i: (0, 0)),
                  pl.BlockSpec((SGU_GROUPS, CHUNK, CHUNK), lambda i: (0, 0, 0)),
                  pl.BlockSpec((CHUNK, SGU_GROUPS), lambda i: (0, 0))],
        out_specs=pl.BlockSpec((rows, width), lambda i: (i, 0)),
        out_shape=jax.ShapeDtypeStruct((M, width), BF16),
        compiler_params=_params("parallel"),
        name=name,
    )(p, p, p, ln_g.reshape(1, width), ln_b.reshape(1, width), w_s, b_s.T)


def _sgu_first_row_kernel(u_ref, v_ref, z_ref, lng_ref, lnb_ref, w0_ref, b0_ref, o_ref, vn_ref):
    vn = _layernorm(jax.nn.gelu(v_ref[...]), lng_ref[...], lnb_ref[...])
    vn_ref[...] = vn
    mixed = w0_ref[...] * vn + b0_ref[...]
    o_ref[...] = (jax.nn.gelu(u_ref[...]) * mixed * jax.nn.silu(z_ref[...])).astype(o_ref.dtype)


def sgu_first_row(p, ln_g, ln_b, w_s, b_s, *, width, name):
    M = p.shape[0]
    ch = width // SGU_GROUPS
    w0 = jnp.repeat(w_s[:, 0, 0], ch).reshape(1, width)
    b0 = jnp.repeat(b_s[:, 0], ch).reshape(1, width)
    row = pl.BlockSpec((1, width), lambda i: (0, 0))
    return pl.pallas_call(
        _sgu_first_row_kernel,
        grid=(1,),
        in_specs=[pl.BlockSpec((M, width), lambda i: (0, 0)),
                  pl.BlockSpec((M, width), lambda i: (0, 1)),
                  pl.BlockSpec((M, width), lambda i: (0, 2)),
                  row, row, row, row],
        out_specs=[pl.BlockSpec((M, width), lambda i: (0, 0)),
                   pl.BlockSpec((M, width), lambda i: (0, 0))],
        out_shape=[jax.ShapeDtypeStruct((M, width), BF16),
                   jax.ShapeDtypeStruct((M, width), F32)],
        compiler_params=_params("arbitrary"),
        name=name,
    )(p, p, p, ln_g.reshape(1, width), ln_b.reshape(1, width), w0, b0)


def _mem_attn_kernel(q_ref, zm_ref, k_ref, v_ref, o_ref):
    tm, width = q_ref.shape
    hd = width // MEM_HEADS
    q = q_ref[...]
    if tm < 8:
        q = jnp.broadcast_to(q[0:1], (8, width))
    for h in range(MEM_HEADS):
        cs = slice(h * hd, (h + 1) * hd)
        s = lax.dot_general(q[:, cs].astype(BF16), k_ref[:, cs].astype(BF16), NT_DIMS,
                            preferred_element_type=F32) * (hd ** -0.5)
        e = jnp.exp(s - jnp.max(s, axis=-1, keepdims=True))
        p = e * (1.0 / jnp.sum(e, axis=-1, keepdims=True))
        o = jnp.dot(p.astype(BF16), v_ref[:, cs].astype(BF16), preferred_element_type=F32)
        o_ref[:, cs] = (o[:tm] * jax.nn.silu(zm_ref[:, cs])).astype(o_ref.dtype)


def mem_attention(p, mem_k, mem_v, *, q_block, tm, name):
    B, L, _ = p.shape
    _, mlen, width = mem_k.shape
    tm = min(tm, L)
    return pl.pallas_call(
        _mem_attn_kernel,
        grid=(B, L // tm),
        in_specs=[pl.BlockSpec((None, tm, width), lambda b, i: (b, i, q_block)),
                  pl.BlockSpec((None, tm, width), lambda b, i: (b, i, q_block + 1)),
                  pl.BlockSpec((None, mlen, width), lambda b, i: (b, 0, 0)),
                  pl.BlockSpec((None, mlen, width), lambda b, i: (b, 0, 0))],
        out_specs=pl.BlockSpec((None, tm, width), lambda b, i: (b, i, 0)),
        out_shape=jax.ShapeDtypeStruct((B, L, width), BF16),
        compiler_params=_params("parallel", "parallel"),
        name=name,
    )(p, p, mem_k, mem_v)


def _rel_bucket(dist):
    n = jnp.maximum(dist, 0)
    max_exact = N_BUCKETS // 2
    nf = jnp.maximum(n, 1).astype(F32)
    large = max_exact + (jnp.log(nf / max_exact) / math.log(MAX_DISTANCE / max_exact)
                         * (N_BUCKETS - max_exact)).astype(jnp.int32)
    large = jnp.minimum(large, N_BUCKETS - 1)
    return jnp.where(n < max_exact, n, large)


def _lambda(lam_ref, lam_init):
    a = jnp.sum(lam_ref[0:1, :] * lam_ref[1:2, :], axis=-1, keepdims=True)
    b = jnp.sum(lam_ref[2:3, :] * lam_ref[3:4, :], axis=-1, keepdims=True)
    return jnp.exp(a) - jnp.exp(b) + lam_init


def _subln_gate(o, g, z, lam_init):
    ms = jnp.mean(o * o, axis=-1, keepdims=True)
    on = (o * lax.rsqrt(ms + EPS) * g) * (1.0 - lam_init)
    return on * jax.nn.silu(z)


def _diff_attn_kernel(q_ref, z_ref, k_ref, v_ref, b0_ref, b1_ref, bfar_ref, lam_ref, g_ref, o_ref,
                      kb_ref, vb_ref, m_ref, l_ref, acc_ref, *, lam_init):
    t = q_ref.shape[0]
    qi = pl.program_id(2)
    scale = B_HD ** -0.5

    @pl.when(qi == 0)
    def _():
        kb_ref[...] = k_ref[...].astype(BF16)
        vb_ref[...] = v_ref[...].astype(BF16)

    q = q_ref[...]
    qs = (q[:, :B_HD].astype(BF16), q[:, B_HD:].astype(BF16))
    m_ref[...] = jnp.full(m_ref.shape, NEG, F32)
    l_ref[...] = jnp.zeros(l_ref.shape, F32)
    acc_ref[...] = jnp.zeros(acc_ref.shape, F32)

    def block(kj, bias):
        off = pl.multiple_of(kj * t, t)
        k = kb_ref[pl.ds(off, t), :]
        v = vb_ref[pl.ds(off, t), :]
        for c in range(2):
            s = lax.dot_general(qs[c], k[:, c * B_HD:(c + 1) * B_HD], NT_DIMS,
                                preferred_element_type=F32) * scale + bias
            m_old = m_ref[c]
            m_new = jnp.maximum(m_old, jnp.max(s, axis=-1, keepdims=True))
            alpha = jnp.exp(m_old - m_new)
            p = jnp.exp(s - m_new)
            l_ref[c] = alpha * l_ref[c] + jnp.sum(p, axis=-1, keepdims=True)
            acc_ref[c] = alpha * acc_ref[c] + jnp.dot(p.astype(BF16), v, preferred_element_type=F32)
            m_ref[c] = m_new

    bfar = bfar_ref[0:1, 0:1]

    def far(kj, carry):
        block(kj, bfar)
        return carry

    lax.fori_loop(0, jnp.maximum(qi - 1, 0), far, 0)

    @pl.when(qi >= 1)
    def _():
        block(qi - 1, b1_ref[...])

    block(qi, b0_ref[...])

    lam = _lambda(lam_ref, lam_init)
    o = acc_ref[0] * (1.0 / l_ref[0]) - lam * (acc_ref[1] * (1.0 / l_ref[1]))
    o_ref[...] = _subln_gate(o, g_ref[...], z_ref[...], lam_init).astype(o_ref.dtype)


def diff_attention_prompt(p, k, v, rel_bias, lam_rows, subln_g, *, lam_init, t, name):
    B, L, _ = p.shape
    width = k.shape[-1]
    H = width // B_HD2
    table = rel_bias.astype(F32)
    d = jnp.arange(t, dtype=jnp.int32)[:, None] - jnp.arange(t, dtype=jnp.int32)[None, :]
    b0 = jnp.where((d >= 0)[None], jnp.moveaxis(table[_rel_bucket(d)], -1, 0), NEG)
    b1 = jnp.moveaxis(table[_rel_bucket(d + t)], -1, 0)
    bfar = jnp.broadcast_to(table[_rel_bucket(jnp.int32(2 * t))][:, None, None], (H, 1, 128))
    assert t >= MAX_DISTANCE and L % t == 0
    blk = lambda off: pl.BlockSpec((None, t, B_HD2), lambda b, h, i: (b, i, off + h))
    kv = pl.BlockSpec((None, L, B_HD2), lambda b, h, i: (b, 0, h))
    tile = pl.BlockSpec((None, t, t), lambda b, h, i: (h, 0, 0))
    return pl.pallas_call(
        functools.partial(_diff_attn_kernel, lam_init=lam_init),
        grid=(B, H, L // t),
        in_specs=[blk(0), blk(H), kv, kv, tile, tile,
                  pl.BlockSpec((None, 1, 128), lambda b, h, i: (h, 0, 0)),
                  pl.BlockSpec((4, B_HD), lambda b, h, i: (0, 0)),
                  pl.BlockSpec((1, B_HD2), lambda b, h, i: (0, 0))],
        out_specs=pl.BlockSpec((None, t, B_HD2), lambda b, h, i: (b, i, h)),
        out_shape=jax.ShapeDtypeStruct((B, L, width), BF16),
        scratch_shapes=[pltpu.VMEM((L, B_HD2), BF16), pltpu.VMEM((L, B_HD2), BF16),
                        pltpu.VMEM((2, t, 1), F32), pltpu.VMEM((2, t, 1), F32),
                        pltpu.VMEM((2, t, B_HD2), F32)],
        compiler_params=_params("parallel", "parallel", "arbitrary"),
        name=name,
    )(p, p, k, v, b0, b1, bfar, lam_rows, subln_g.reshape(1, B_HD2))


def _decode_attn_kernel(pt_ref, q_ref, z_ref, kn_ref, vn_ref, sb_ref, lam_ref, g_ref, *rest,
                        pps, lam_init):
    del pt_ref
    k_pages, v_pages = rest[:pps], rest[pps:2 * pps]
    o_ref, qb_ref, m_ref, l_ref, acc_ref = rest[2 * pps:]
    nr, width = qb_ref.shape
    H = width // B_HD2
    j = pl.program_id(1)
    last = pl.num_programs(1) - 1
    scale = B_HD ** -0.5
    r_id = lax.broadcasted_iota(jnp.int32, (nr, width), 0)
    c_id = lax.broadcasted_iota(jnp.int32, (nr, width), 1)
    own = (c_id >> 7) == r_id
    q_rows = jnp.where(own, q_ref[...], 0.0)

    @pl.when(j == 0)
    def _():
        qb_ref[...] = q_rows.astype(BF16)
        m_ref[...] = jnp.full(m_ref.shape, NEG, F32)
        l_ref[...] = jnp.zeros(l_ref.shape, F32)
        acc_ref[...] = jnp.zeros(acc_ref.shape, F32)

    def update(s, pv_fn):
        m_old = m_ref[...]
        m_new = jnp.maximum(m_old, jnp.max(s, axis=-1, keepdims=True))
        alpha = jnp.exp(m_old - m_new)
        p = jnp.exp(s - m_new)
        l_ref[...] = alpha * l_ref[...] + jnp.sum(p, axis=-1, keepdims=True)
        acc_ref[...] = alpha * acc_ref[...] + pv_fn(p)
        m_ref[...] = m_new

    for r in range(pps):
        bias = sb_ref[0]
        if r == pps - 1:
            bias = jnp.where(j == last, sb_ref[1], bias)
        s = lax.dot_general(qb_ref[...], k_pages[r][...].astype(BF16), NT_DIMS,
                            preferred_element_type=F32) * scale + bias
        vb = v_pages[r][...].astype(BF16)
        update(s, lambda p, vb=vb: jnp.dot(p.astype(BF16), vb, preferred_element_type=F32))

    @pl.when(j == last)
    def _():
        s = jnp.sum(q_rows * kn_ref[...], axis=-1, keepdims=True) * scale + sb_ref[2][:, 0:1]
        vn = vn_ref[...]
        update(s, lambda p: p * vn)
        lam = _lambda(lam_ref, lam_init)
        o_all = acc_ref[...] * (1.0 / l_ref[...])
        for h in range(H):
            cs = slice(h * B_HD2, (h + 1) * B_HD2)
            o = o_all[2 * h:2 * h + 1, cs] - lam * o_all[2 * h + 1:2 * h + 2, cs]
            o_ref[:, cs] = _subln_gate(o, g_ref[...], z_ref[:, cs], lam_init).astype(o_ref.dtype)


def diff_attention_decode(p, k_new, v_new, cache_k, cache_v, page_table, rel_bias, lam_rows, subln_g,
                          *, lam_init, pps, name):
    B = p.shape[0]
    width = k_new.shape[-1]
    H = width // B_HD2
    n_pages = page_table.shape[1]
    assert n_pages % pps == 0 and PAGE_SIZE >= MAX_DISTANCE
    nr = -(-2 * H // 8) * 8
    table = rel_bias.astype(F32)
    per_row = lambda x: jnp.pad(jnp.repeat(x, 2, axis=0), ((0, nr - 2 * H), (0, 0)))
    dist_last = PAGE_SIZE - jnp.arange(PAGE_SIZE, dtype=jnp.int32)
    sb = jnp.stack([
        per_row(jnp.broadcast_to(table[_rel_bucket(jnp.int32(PAGE_SIZE + 1))][:, None], (H, PAGE_SIZE))),
        per_row(table[_rel_bucket(dist_last)].T),
        per_row(jnp.broadcast_to(table[_rel_bucket(jnp.int32(0))][:, None], (H, PAGE_SIZE))),
    ])
    row = lambda c: pl.BlockSpec((None, 1, width), lambda b, j, pt: (b, 0, c))
    page = lambda r: pl.BlockSpec((None, PAGE_SIZE, width),
                                  lambda b, j, pt: (pt[b * n_pages + j * pps + r], 0, 0))
    grid_spec = pltpu.PrefetchScalarGridSpec(
        num_scalar_prefetch=1,
        grid=(B, n_pages // pps),
        in_specs=[row(0), row(1), row(0), row(0),
                  pl.BlockSpec((3, nr, PAGE_SIZE), lambda b, j, pt: (0, 0, 0)),
                  pl.BlockSpec((4, B_HD), lambda b, j, pt: (0, 0)),
                  pl.BlockSpec((1, B_HD2), lambda b, j, pt: (0, 0))]
                 + [page(r) for r in range(pps)] * 2,
        out_specs=row(0),
        scratch_shapes=[pltpu.VMEM((nr, width), BF16), pltpu.VMEM((nr, 1), F32),
                        pltpu.VMEM((nr, 1), F32), pltpu.VMEM((nr, width), F32)],
    )
    return pl.pallas_call(
        functools.partial(_decode_attn_kernel, pps=pps, lam_init=lam_init),
        grid_spec=grid_spec,
        out_shape=jax.ShapeDtypeStruct((B, 1, width), BF16),
        compiler_params=_params("parallel", "arbitrary"),
        name=name,
    )(page_table.reshape(-1), p, p, k_new, v_new, sb, lam_rows, subln_g.reshape(1, B_HD2),
      *([cache_k] * pps), *([cache_v] * pps))


def _run_group(x3, mem_k, mem_v, past, W, *, prompt):
    B, L, D = x3.shape
    M = B * L
    depth = W["norm_g"].shape[0]
    n_a = W["w_in_a"].shape[0]
    e_a = W["sgu_ln_g"].shape[1]
    b_width = W["w_k"].shape[1]
    mem_width = mem_k.shape[-1]
    H = b_width // B_HD2
    tag = "p" if prompt else "s"
    tm, tn = (512, 1024) if prompt else (M, 1024)
    x = x3.reshape(M, D)
    chunk_vs = []
    k_new = v_new = None
    for l in range(depth):
        if l < n_a:
            p = rms_matmul(x, W["norm_g"][l], W["w_in_a"][l], tm=tm, tn=tn, name=f"in_a{l}_{tag}")
            if prompt:
                main = sgu_prompt(p, W["sgu_ln_g"][l], W["sgu_ln_b"][l], W["sgu_w"][l], W["sgu_b"][l],
                                  width=e_a, rows=2 * CHUNK, name=f"sgu{l}_{tag}")
            else:
                main, vn = sgu_first_row(p, W["sgu_ln_g"][l], W["sgu_ln_b"][l], W["sgu_w"][l], W["sgu_b"][l],
                                         width=e_a, name=f"sgu{l}_{tag}")
                chunk_vs.append(vn)
            q_block = 3 * e_a // mem_width
            w_out = W["w_out_a"][l]
        else:
            jb = l - n_a
            p = rms_matmul(x, W["norm_g"][l], W["w_in_b"][jb], tm=tm, tn=tn, name=f"in_b{jb}_{tag}")
            lam_init = 0.8 - 0.6 * math.exp(-0.3 * l)
            lam_rows = jnp.stack([W["lambda_q1"][jb], W["lambda_k1"][jb], W["lambda_q2"][jb], W["lambda_k2"][jb]])
            if prompt:
                main = diff_attention_prompt(p.reshape(B, L, -1), k_new.reshape(B, L, -1), v_new.reshape(B, L, -1),
                                             W["rel_bias"], lam_rows, W["subln_g"][jb],
                                             lam_init=lam_init, t=256, name=f"dattn{jb}_{tag}")
            else:
                main = diff_attention_decode(p.reshape(B, L, -1), k_new.reshape(B, L, -1), v_new.reshape(B, L, -1),
                                             *past, W["rel_bias"], lam_rows, W["subln_g"][jb],
                                             lam_init=lam_init, pps=4, name=f"dattn{jb}_{tag}")
            main = main.reshape(M, b_width)
            q_block = 2 * b_width // mem_width
            w_out = W["w_out_b"][jb]
        mem_o = mem_attention(p.reshape(B, L, -1), mem_k[l], mem_v[l], q_block=q_block, tm=512,
                              name=f"mem{l}_{tag}")
        x = out_proj(main, mem_o.reshape(M, mem_width), w_out, x, tm=tm, tn=tn, name=f"out{l}_{tag}")
        if l == n_a - 1:
            k_new = rms_matmul(x, W["kv_norm_g"], W["w_k"], tm=tm, tn=tn, name=f"k_{tag}")
            v_new = rms_matmul(x, W["kv_norm_g"], W["w_v"], tm=tm, tn=tn, name=f"v_{tag}")
    y = rmsnorm_rows(x, W["final_norm_g"], tm=256, name=f"final_{tag}")
    return (y.reshape(B, L, D), k_new.reshape(B, L, H, 2, B_HD), v_new.reshape(B, L, H, B_HD2), chunk_vs)


def kernel(x_prompt, x_sample, cache_k, cache_v, cache_mem_k, cache_mem_v, page_table, mem_prompt,
           norm_g, final_norm_g, mem_norm_g, w_mem_kv, w_in_a, sgu_ln_g, sgu_ln_b, sgu_w, sgu_b,
           w_out_a, kv_norm_g, w_kv, w_in_b, lambda_q1, lambda_k1, lambda_q2, lambda_k2,
           subln_g, rel_bias, w_out_b):
    depth = norm_g.shape[0]
    b_width = w_kv.shape[1] // 2
    mem_width = w_mem_kv.shape[-1] // 2
    mem_hd = mem_width // MEM_HEADS
    W = dict(norm_g=norm_g, final_norm_g=final_norm_g, sgu_ln_g=sgu_ln_g, sgu_ln_b=sgu_ln_b, sgu_w=sgu_w,
             sgu_b=sgu_b, kv_norm_g=kv_norm_g, lambda_q1=lambda_q1, lambda_k1=lambda_k1, lambda_q2=lambda_q2,
             lambda_k2=lambda_k2, subln_g=subln_g, rel_bias=rel_bias,
             w_in_a=w_in_a.astype(BF16), w_out_a=w_out_a.astype(BF16), w_in_b=w_in_b.astype(BF16),
             w_out_b=w_out_b.astype(BF16), w_k=w_kv[:, :b_width].astype(BF16), w_v=w_kv[:, b_width:].astype(BF16))

    bp, m_len, d_model = mem_prompt.shape
    mkv = rms_matmul_stacked(mem_prompt.reshape(bp * m_len, d_model), mem_norm_g, w_mem_kv.astype(BF16),
                             tm=512, tn=mem_width, name="mem_kv")
    mkv = mkv.reshape(depth, 2, bp, m_len, mem_width)
    mem_k_p, mem_v_p = mkv[:, 0], mkv[:, 1]
    y_p, k_p, v_p, _ = _run_group(x_prompt, mem_k_p, mem_v_p, None, W, prompt=True)

    bd = x_sample.shape[0]
    n_pool, page_size = cache_k.shape[:2]
    past = (cache_k.reshape(n_pool, page_size, b_width), cache_v.reshape(n_pool, page_size, b_width), page_table)
    smk = cache_mem_k.reshape(depth, bd, -1, mem_width)
    smv = cache_mem_v.reshape(depth, bd, -1, mem_width)
    y_s, k_s, v_s, chunk_vs = _run_group(x_sample, smk, smv, past, W, prompt=False)
    chunk_v_s = jnp.stack(chunk_vs).reshape(len(chunk_vs), bd, x_sample.shape[1], -1)

    mem_shape = (depth, bp, m_len, MEM_HEADS, mem_hd)
    return (y_p, y_s, k_p, v_p, k_s, v_s, mem_k_p.reshape(mem_shape), mem_v_p.reshape(mem_shape), chunk_v_s)
```

```python
import functools
import math

import jax
import jax.numpy as jnp
from jax import lax
from jax.experimental import pallas as pl
from jax.experimental.pallas import tpu as pltpu

F32 = jnp.float32
BF16 = jnp.bfloat16

CHUNK = 128
SGU_GROUPS = 8
B_HD2 = 256
B_HD = B_HD2 // 2
MEM_HEADS = 4
N_BUCKETS = 32
MAX_DISTANCE = 128
PAGE_SIZE = 128
EPS = 1e-6
NEG = -1e30

VMEM_BYTES_V7X = 64 * 1024 * 1024
VMEM_LIMIT = VMEM_BYTES_V7X - 8 * 1024 * 1024

NT_DIMS = (((1,), (1,)), ((), ()))


def _params(*sem):
    return pltpu.CompilerParams(dimension_semantics=sem, vmem_limit_bytes=VMEM_LIMIT)


def _normalised_rows(x_ref, g_ref, h_ref):
    @pl.when(pl.program_id(1) == 0)
    def _():
        x = x_ref[...]
        ms = jnp.mean(x * x, axis=-1, keepdims=True)
        h_ref[...] = (x * lax.rsqrt(ms + EPS) * g_ref[...]).astype(BF16)

    return h_ref[...]


def _rms_matmul_kernel(x_ref, g_ref, w_ref, o_ref, h_ref):
    h = _normalised_rows(x_ref, g_ref, h_ref)
    o_ref[...] = jnp.dot(h, w_ref[...], preferred_element_type=F32).astype(o_ref.dtype)


def rms_matmul(x, g, w, *, tm, tn, name):
    M, D = x.shape
    N = w.shape[1]
    tm, tn = min(tm, M), min(tn, N)
    return pl.pallas_call(
        _rms_matmul_kernel,
        grid=(M // tm, N // tn),
        in_specs=[pl.BlockSpec((tm, D), lambda i, j: (i, 0)),
                  pl.BlockSpec((1, D), lambda i, j: (0, 0)),
                  pl.BlockSpec((D, tn), lambda i, j: (0, j))],
        out_specs=pl.BlockSpec((tm, tn), lambda i, j: (i, j)),
        out_shape=jax.ShapeDtypeStruct((M, N), F32),
        scratch_shapes=[pltpu.VMEM((tm, D), BF16)],
        compiler_params=_params("parallel", "arbitrary"),
        name=name,
    )(x, g.reshape(1, D), w)


def rms_matmul_stacked(x, g, w, *, tm, tn, name):
    M, D = x.shape
    G, _, N = w.shape
    nb = N // tn
    return pl.pallas_call(
        _rms_matmul_kernel,
        grid=(M // tm, G * nb),
        in_specs=[pl.BlockSpec((tm, D), lambda i, j: (i, 0)),
                  pl.BlockSpec((1, D), lambda i, j: (0, 0)),
                  pl.BlockSpec((None, D, tn), lambda i, j: (j // nb, 0, j % nb))],
        out_specs=pl.BlockSpec((None, tm, tn), lambda i, j: (j, i, 0)),
        out_shape=jax.ShapeDtypeStruct((G * nb, M, tn), F32),
        scratch_shapes=[pltpu.VMEM((tm, D), BF16)],
        compiler_params=_params("parallel", "arbitrary"),
        name=name,
    )(x, g.reshape(1, D), w)


def _k_proj_kernel(x_ref, g_ref, w_ref, knat_ref, katt_ref, h_ref):
    tm = x_ref.shape[0]
    heads = katt_ref.shape[0]
    j = pl.program_id(1)
    h = _normalised_rows(x_ref, g_ref, h_ref)
    res = jnp.dot(h, w_ref[...], preferred_element_type=F32)
    n_sub = knat_ref.shape[0] // tm
    for q in range(2 * heads):
        knat_ref[pl.ds(j * 2 * heads + q, tm, stride=n_sub), :] = res[:, q * B_HD:(q + 1) * B_HD]
    for hh in range(heads):
        katt_ref[hh] = res[:, hh * B_HD2:(hh + 1) * B_HD2].astype(BF16)


def k_proj_prompt(x, g, w, *, B, tm, heads, name):
    M, D = x.shape
    L = M // B
    N = w.shape[1]
    H = N // B_HD2
    nlb = L // tm
    return pl.pallas_call(
        _k_proj_kernel,
        grid=(M // tm, H // heads),
        in_specs=[pl.BlockSpec((tm, D), lambda i, j: (i, 0)),
                  pl.BlockSpec((1, D), lambda i, j: (0, 0)),
                  pl.BlockSpec((D, heads * B_HD2), lambda i, j: (0, j))],
        out_specs=[pl.BlockSpec((tm * 2 * H, B_HD), lambda i, j: (i, 0)),
                   pl.BlockSpec((None, heads, tm, B_HD2), lambda i, j: (i // nlb, j, i % nlb, 0))],
        out_shape=[jax.ShapeDtypeStruct((M * 2 * H, B_HD), F32),
                   jax.ShapeDtypeStruct((B, H, L, B_HD2), BF16)],
        scratch_shapes=[pltpu.VMEM((tm, D), BF16)],
        compiler_params=_params("parallel", "arbitrary"),
        name=name,
    )(x, g.reshape(1, D), w)


def _v_proj_kernel(x_ref, g_ref, w_ref, v_ref, h_ref):
    h = _normalised_rows(x_ref, g_ref, h_ref)
    res = jnp.dot(h, w_ref[...], preferred_element_type=F32)
    for hh in range(v_ref.shape[0]):
        v_ref[hh] = res[:, hh * B_HD2:(hh + 1) * B_HD2]


def v_proj_prompt(x, g, w, *, B, tm, heads, name):
    M, D = x.shape
    L = M // B
    H = w.shape[1] // B_HD2
    nlb = L // tm
    return pl.pallas_call(
        _v_proj_kernel,
        grid=(M // tm, H // heads),
        in_specs=[pl.BlockSpec((tm, D), lambda i, j: (i, 0)),
                  pl.BlockSpec((1, D), lambda i, j: (0, 0)),
                  pl.BlockSpec((D, heads * B_HD2), lambda i, j: (0, j))],
        out_specs=pl.BlockSpec((None, heads, tm, B_HD2), lambda i, j: (i // nlb, j, i % nlb, 0)),
        out_shape=jax.ShapeDtypeStruct((B, H, L, B_HD2), F32),
        scratch_shapes=[pltpu.VMEM((tm, D), BF16)],
        compiler_params=_params("parallel", "arbitrary"),
        name=name,
    )(x, g.reshape(1, D), w)


def _out_proj_kernel(a_ref, m_ref, w_ref, x_ref, o_ref):
    ka = a_ref.shape[1]
    acc = jnp.dot(a_ref[...], w_ref[:ka, :], preferred_element_type=F32)
    acc = acc + jnp.dot(m_ref[...], w_ref[ka:, :], preferred_element_type=F32)
    o_ref[...] = x_ref[...] + acc


def out_proj(a, m, w, x, *, tm, tn, name):
    M, ka = a.shape
    km = m.shape[1]
    N = w.shape[1]
    tm, tn = min(tm, M), min(tn, N)
    return pl.pallas_call(
        _out_proj_kernel,
        grid=(M // tm, N // tn),
        in_specs=[pl.BlockSpec((tm, ka), lambda i, j: (i, 0)),
                  pl.BlockSpec((tm, km), lambda i, j: (i, 0)),
                  pl.BlockSpec((ka + km, tn), lambda i, j: (0, j)),
                  pl.BlockSpec((tm, tn), lambda i, j: (i, j))],
        out_specs=pl.BlockSpec((tm, tn), lambda i, j: (i, j)),
        out_shape=jax.ShapeDtypeStruct((M, N), F32),
        compiler_params=_params("parallel", "arbitrary"),
        name=name,
    )(a, m, w, x)


def _rmsnorm_kernel(x_ref, g_ref, o_ref):
    x = x_ref[...]
    ms = jnp.mean(x * x, axis=-1, keepdims=True)
    o_ref[...] = x * lax.rsqrt(ms + EPS) * g_ref[...]


def rmsnorm_rows(x, g, *, tm, name):
    M, D = x.shape
    tm = min(tm, M)
    return pl.pallas_call(
        _rmsnorm_kernel,
        grid=(M // tm,),
        in_specs=[pl.BlockSpec((tm, D), lambda i: (i, 0)),
                  pl.BlockSpec((1, D), lambda i: (0, 0))],
        out_specs=pl.BlockSpec((tm, D), lambda i: (i, 0)),
        out_shape=jax.ShapeDtypeStruct((M, D), F32),
        compiler_params=_params("parallel"),
        name=name,
    )(x, g.reshape(1, D))


def _layernorm(v, g, b):
    mu = jnp.mean(v, axis=-1, keepdims=True)
    vc = v - mu
    var = jnp.mean(vc * vc, axis=-1, keepdims=True)
    return vc * lax.rsqrt(var + EPS) * g + b


def _sgu_kernel(u_ref, v_ref, z_ref, lng_ref, lnb_ref, w_ref, bt_ref, o_ref):
    rows, width = u_ref.shape
    ch = width // SGU_GROUPS
    r = lax.broadcasted_iota(jnp.int32, (CHUNK, CHUNK), 0)
    c = lax.broadcasted_iota(jnp.int32, (CHUNK, CHUNK), 1)
    causal = c <= r
    ws = [jnp.where(causal, w_ref[g], 0.0).astype(BF16) for g in range(SGU_GROUPS)]
    for n in range(rows // CHUNK):
        rs = slice(n * CHUNK, (n + 1) * CHUNK)
        vn = _layernorm(jax.nn.gelu(v_ref[rs, :]), lng_ref[...], lnb_ref[...]).astype(BF16)
        for g in range(SGU_GROUPS):
            cs = slice(g * ch, (g + 1) * ch)
            mixed = jnp.dot(ws[g], vn[:, cs], preferred_element_type=F32) + bt_ref[:, g:g + 1]
            u = jax.nn.gelu(u_ref[rs, cs])
            o_ref[rs, cs] = (u * mixed * jax.nn.silu(z_ref[rs, cs])).astype(o_ref.dtype)


def sgu_prompt(p, ln_g, ln_b, w_s, b_s, *, width, rows, name):
    M = p.shape[0]
    return pl.pallas_call(
        _sgu_kernel,
        grid=(M // rows,),
        in_specs=[pl.BlockSpec((rows, width), lambda i: (i, 0)),
                  pl.BlockSpec((rows, width), lambda i: (i, 1)),
                  pl.BlockSpec((rows, width), lambda i: (i, 2)),
                  pl.BlockSpec((1, width), lambda i: (0, 0)),
                  pl.BlockSpec((1, width), lambda i: (0, 0)),
                  pl.BlockSpec((SGU_GROUPS, CHUNK, CHUNK), lambda i: (0, 0, 0)),
                  pl.BlockSpec((CHUNK, SGU_GROUPS), lambda i: (0, 0))],
        out_specs=pl.BlockSpec((rows, width), lambda i: (i, 0)),
        out_shape=jax.ShapeDtypeStruct((M, width), BF16),
        compiler_params=_params("parallel"),
        name=name,
    )(p, p, p, ln_g.reshape(1, width), ln_b.reshape(1, width), w_s, b_s.T)


def _sgu_first_row_kernel(u_ref, v_ref, z_ref, lng_ref, lnb_ref, w0_ref, b0_ref, o_ref, vn_ref):
    vn = _layernorm(jax.nn.gelu(v_ref[...]), lng_ref[...], lnb_ref[...])
    vn_ref[...] = vn
    mixed = w0_ref[...] * vn + b0_ref[...]
    o_ref[...] = (jax.nn.gelu(u_ref[...]) * mixed * jax.nn.silu(z_ref[...])).astype(o_ref.dtype)


def sgu_first_row(p, ln_g, ln_b, w_s, b_s, *, width, name):
    M = p.shape[0]
    ch = width // SGU_GROUPS
    w0 = jnp.repeat(w_s[:, 0, 0], ch).reshape(1, width)
    b0 = jnp.repeat(b_s[:, 0], ch).reshape(1, width)
    row = pl.BlockSpec((1, width), lambda i: (0, 0))
    return pl.pallas_call(
        _sgu_first_row_kernel,
        grid=(1,),
        in_specs=[pl.BlockSpec((M, width), lambda i: (0, 0)),
                  pl.BlockSpec((M, width), lambda i: (0, 1)),
                  pl.BlockSpec((M, width), lambda i: (0, 2)),
                  row, row, row, row],
        out_specs=[pl.BlockSpec((M, width), lambda i: (0, 0)),
                   pl.BlockSpec((M, width), lambda i: (0, 0))],
        out_shape=[jax.ShapeDtypeStruct((M, width), BF16),
                   jax.ShapeDtypeStruct((M, width), F32)],
        compiler_params=_params("arbitrary"),
        name=name,
    )(p, p, p, ln_g.reshape(1, width), ln_b.reshape(1, width), w0, b0)


def _mem_attn_kernel(q_ref, zm_ref, k_ref, v_ref, o_ref):
    tm, width = q_ref.shape
    hd = width // MEM_HEADS
    q = q_ref[...]
    if tm < 8:
        q = jnp.broadcast_to(q[0:1], (8, width))
    for h in range(MEM_HEADS):
        cs = slice(h * hd, (h + 1) * hd)
        s = lax.dot_general(q[:, cs].astype(BF16), k_ref[:, cs].astype(BF16), NT_DIMS,
                            preferred_element_type=F32) * (hd ** -0.5)
        e = jnp.exp(s - jnp.max(s, axis=-1, keepdims=True))
        p = e * (1.0 / jnp.sum(e, axis=-1, keepdims=True))
        o = jnp.dot(p.astype(BF16), v_ref[:, cs].astype(BF16), preferred_element_type=F32)
        o_ref[:, cs] = (o[:tm] * jax.nn.silu(zm_ref[:, cs])).astype(o_ref.dtype)


def mem_attention(p, mem_k, mem_v, *, q_block, tm, name):
    B, L, _ = p.shape
    _, mlen, width = mem_k.shape
    tm = min(tm, L)
    return pl.pallas_call(
        _mem_attn_kernel,
        grid=(B, L // tm),
        in_specs=[pl.BlockSpec((None, tm, width), lambda b, i: (b, i, q_block)),
                  pl.BlockSpec((None, tm, width), lambda b, i: (b, i, q_block + 1)),
                  pl.BlockSpec((None, mlen, width), lambda b, i: (b, 0, 0)),
                  pl.BlockSpec((None, mlen, width), lambda b, i: (b, 0, 0))],
        out_specs=pl.BlockSpec((None, tm, width), lambda b, i: (b, i, 0)),
        out_shape=jax.ShapeDtypeStruct((B, L, width), BF16),
        compiler_params=_params("parallel", "parallel"),
        name=name,
    )(p, p, mem_k, mem_v)


def _rel_bucket(dist):
    n = jnp.maximum(dist, 0)
    max_exact = N_BUCKETS // 2
    nf = jnp.maximum(n, 1).astype(F32)
    large = max_exact + (jnp.log(nf / max_exact) / math.log(MAX_DISTANCE / max_exact)
                         * (N_BUCKETS - max_exact)).astype(jnp.int32)
    large = jnp.minimum(large, N_BUCKETS - 1)
    return jnp.where(n < max_exact, n, large)


def _lambda(lam_ref, lam_init):
    a = jnp.sum(lam_ref[0:1, :] * lam_ref[1:2, :], axis=-1, keepdims=True)
    b = jnp.sum(lam_ref[2:3, :] * lam_ref[3:4, :], axis=-1, keepdims=True)
    return jnp.exp(a) - jnp.exp(b) + lam_init


def _subln_gate(o, g, z, lam_init):
    ms = jnp.mean(o * o, axis=-1, keepdims=True)
    on = (o * lax.rsqrt(ms + EPS) * g) * (1.0 - lam_init)
    return on * jax.nn.silu(z)


def _lane_groups(x):
    return [x[:, g * 128:(g + 1) * 128] for g in range(x.shape[1] // 128)]


def _diff_attn_kernel(q_ref, z_ref, k_ref, v_ref, b0_ref, b1_ref, bfar_ref, lam_ref, g_ref, o_ref,
                      vb_ref, s_ref, mp_ref, lp_ref, acc_ref, *, lam_init):
    t = q_ref.shape[0]
    qi = pl.program_id(2)
    scale = B_HD ** -0.5

    @pl.when(qi == 0)
    def _():
        vb_ref[...] = v_ref[...].astype(BF16)

    q = q_ref[...]
    qs = (q[:, :B_HD].astype(BF16), q[:, B_HD:].astype(BF16))
    mp_ref[...] = jnp.full(mp_ref.shape, NEG, F32)
    lp_ref[...] = jnp.zeros(lp_ref.shape, F32)
    acc_ref[...] = jnp.zeros(acc_ref.shape, F32)

    def scores(kj, bias):
        off = pl.multiple_of(kj * t, t)
        k = k_ref[pl.ds(off, t), :]
        for c in range(2):
            s = lax.dot_general(qs[c], k[:, c * B_HD:(c + 1) * B_HD], NT_DIMS,
                                preferred_element_type=F32) * scale + bias
            s_ref[c, :, pl.ds(off, t)] = s
            mp_ref[c] = functools.reduce(jnp.maximum, _lane_groups(s), mp_ref[c])

    bfar = bfar_ref[0:1, 0:1]

    def far(kj, carry):
        scores(kj, bfar)
        return carry

    lax.fori_loop(0, jnp.maximum(qi - 1, 0), far, 0)

    @pl.when(qi >= 1)
    def _():
        scores(qi - 1, b1_ref[...])

    scores(qi, b0_ref[...])

    m_rows = [jnp.broadcast_to(jnp.max(mp_ref[c], axis=-1, keepdims=True), (t, 128)) for c in range(2)]

    def accumulate(kj, carry):
        off = pl.multiple_of(kj * t, t)
        v = vb_ref[pl.ds(off, t), :]
        for c in range(2):
            ps = [jnp.exp(sg - m_rows[c]) for sg in _lane_groups(s_ref[c, :, pl.ds(off, t)])]
            lp_ref[c] = functools.reduce(jnp.add, ps, lp_ref[c])
            p = jnp.concatenate(ps, axis=1).astype(BF16)
            acc_ref[c] = acc_ref[c] + jnp.dot(p, v, preferred_element_type=F32)
        return carry

    lax.fori_loop(0, qi + 1, accumulate, 0)

    lam = _lambda(lam_ref, lam_init)
    inv_l = [1.0 / jnp.sum(lp_ref[c], axis=-1, keepdims=True) for c in range(2)]
    o = acc_ref[0] * inv_l[0] - lam * (acc_ref[1] * inv_l[1])
    o_ref[...] = _subln_gate(o, g_ref[...], z_ref[...], lam_init).astype(o_ref.dtype)


def _toeplitz_tiles(g, t):
    H, P = g.shape
    hankel = jnp.tile(g[:, ::-1], (1, t + 1))[:, :t * (P + 1)].reshape(H, t, P + 1)
    return hankel[:, ::-1, :t]


def diff_attention_prompt(p, k, v, rel_bias, lam_rows, subln_g, *, lam_init, t, name):
    B, L, _ = p.shape
    H = k.shape[1]
    assert t >= MAX_DISTANCE and L % t == 0
    table = rel_bias.astype(F32)
    d = jnp.arange(-(t - 1), t, dtype=jnp.int32)
    b0 = _toeplitz_tiles(jnp.where((d >= 0)[None], table[_rel_bucket(d)].T, NEG), t)
    b1 = _toeplitz_tiles(table[_rel_bucket(d + t)].T, t)
    bfar = jnp.broadcast_to(table[_rel_bucket(jnp.int32(2 * t))][:, None, None], (H, 1, 128))
    blk = lambda off: pl.BlockSpec((None, t, B_HD2), lambda b, h, i: (b, i, off + h))
    kv = pl.BlockSpec((None, None, L, B_HD2), lambda b, h, i: (b, h, 0, 0))
    tile = pl.BlockSpec((None, t, t), lambda b, h, i: (h, 0, 0))
    return pl.pallas_call(
        functools.partial(_diff_attn_kernel, lam_init=lam_init),
        grid=(B, H, L // t),
        in_specs=[blk(0), blk(H), kv, kv, tile, tile,
                  pl.BlockSpec((None, 1, 128), lambda b, h, i: (h, 0, 0)),
                  pl.BlockSpec((4, B_HD), lambda b, h, i: (0, 0)),
                  pl.BlockSpec((1, B_HD2), lambda b, h, i: (0, 0))],
        out_specs=pl.BlockSpec((None, t, B_HD2), lambda b, h, i: (b, i, h)),
        out_shape=jax.ShapeDtypeStruct((B, L, H * B_HD2), BF16),
        scratch_shapes=[pltpu.VMEM((L, B_HD2), BF16), pltpu.VMEM((2, t, L), F32),
                        pltpu.VMEM((2, t, 128), F32), pltpu.VMEM((2, t, 128), F32),
                        pltpu.VMEM((2, t, B_HD2), F32)],
        compiler_params=_params("parallel", "parallel", "arbitrary"),
        name=name,
    )(p, p, k, v, b0, b1, bfar, lam_rows, subln_g.reshape(1, B_HD2))


def _decode_attn_kernel(pt_ref, q_ref, z_ref, kn_ref, vn_ref, sb_ref, lam_ref, g_ref, *rest,
                        pps, lam_init):
    del pt_ref
    k_pages, v_pages = rest[:pps], rest[pps:2 * pps]
    o_ref, qb_ref, m_ref, l_ref, acc_ref = rest[2 * pps:]
    nr, width = qb_ref.shape
    H = width // B_HD2
    j = pl.program_id(1)
    last = pl.num_programs(1) - 1
    scale = B_HD ** -0.5
    r_id = lax.broadcasted_iota(jnp.int32, (nr, width), 0)
    c_id = lax.broadcasted_iota(jnp.int32, (nr, width), 1)
    own = (c_id >> 7) == r_id
    q_rows = jnp.where(own, q_ref[...], 0.0)

    @pl.when(j == 0)
    def _():
        qb_ref[...] = q_rows.astype(BF16)
        m_ref[...] = jnp.full(m_ref.shape, NEG, F32)
        l_ref[...] = jnp.zeros(l_ref.shape, F32)
        acc_ref[...] = jnp.zeros(acc_ref.shape, F32)

    def update(s, pv_fn):
        m_old = m_ref[...]
        m_new = jnp.maximum(m_old, jnp.max(s, axis=-1, keepdims=True))
        alpha = jnp.exp(m_old - m_new)
        p = jnp.exp(s - m_new)
        l_ref[...] = alpha * l_ref[...] + jnp.sum(p, axis=-1, keepdims=True)
        acc_ref[...] = alpha * acc_ref[...] + pv_fn(p)
        m_ref[...] = m_new

    for r in range(pps):
        bias = sb_ref[0]
        if r == pps - 1:
            bias = jnp.where(j == last, sb_ref[1], bias)
        kt = jnp.concatenate([k_pages[r][pl.ds(hc, PAGE_SIZE, stride=2 * H), :] for hc in range(2 * H)],
                             axis=1).astype(BF16)
        vb = jnp.concatenate([v_pages[r][h] for h in range(H)], axis=1).astype(BF16)
        s = lax.dot_general(qb_ref[...], kt, NT_DIMS, preferred_element_type=F32) * scale + bias
        update(s, lambda p, vb=vb: jnp.dot(p.astype(BF16), vb, preferred_element_type=F32))

    @pl.when(j == last)
    def _():
        s = jnp.sum(q_rows * kn_ref[...], axis=-1, keepdims=True) * scale + sb_ref[2][:, 0:1]
        vn = vn_ref[...]
        update(s, lambda p: p * vn)
        lam = _lambda(lam_ref, lam_init)
        o_all = acc_ref[...] * (1.0 / l_ref[...])
        for h in range(H):
            cs = slice(h * B_HD2, (h + 1) * B_HD2)
            o = o_all[2 * h:2 * h + 1, cs] - lam * o_all[2 * h + 1:2 * h + 2, cs]
            o_ref[:, cs] = _subln_gate(o, g_ref[...], z_ref[:, cs], lam_init).astype(o_ref.dtype)


def diff_attention_decode(p, k_new, v_new, cache_k, cache_v, page_table, rel_bias, lam_rows, subln_g,
                          *, lam_init, pps, name):
    B = p.shape[0]
    width = k_new.shape[-1]
    H = width // B_HD2
    n_pages = page_table.shape[1]
    k_rows = PAGE_SIZE * 2 * H
    assert n_pages % pps == 0 and PAGE_SIZE >= MAX_DISTANCE
    nr = -(-2 * H // 8) * 8
    table = rel_bias.astype(F32)
    per_row = lambda x: jnp.pad(jnp.repeat(x, 2, axis=0), ((0, nr - 2 * H), (0, 0)))
    dist_last = PAGE_SIZE - jnp.arange(PAGE_SIZE, dtype=jnp.int32)
    sb = jnp.stack([
        per_row(jnp.broadcast_to(table[_rel_bucket(jnp.int32(PAGE_SIZE + 1))][:, None], (H, PAGE_SIZE))),
        per_row(table[_rel_bucket(dist_last)].T),
        per_row(jnp.broadcast_to(table[_rel_bucket(jnp.int32(0))][:, None], (H, PAGE_SIZE))),
    ])
    row = lambda c: pl.BlockSpec((None, 1, width), lambda b, j, pt: (b, 0, c))
    k_page = lambda r: pl.BlockSpec((k_rows, B_HD), lambda b, j, pt: (pt[b * n_pages + j * pps + r], 0))
    v_page = lambda r: pl.BlockSpec((None, H, PAGE_SIZE, B_HD2),
                                    lambda b, j, pt: (pt[b * n_pages + j * pps + r], 0, 0, 0))
    grid_spec = pltpu.PrefetchScalarGridSpec(
        num_scalar_prefetch=1,
        grid=(B, n_pages // pps),
        in_specs=[row(0), row(1), row(0), row(0),
                  pl.BlockSpec((3, nr, PAGE_SIZE), lambda b, j, pt: (0, 0, 0)),
                  pl.BlockSpec((4, B_HD), lambda b, j, pt: (0, 0)),
                  pl.BlockSpec((1, B_HD2), lambda b, j, pt: (0, 0))]
                 + [k_page(r) for r in range(pps)] + [v_page(r) for r in range(pps)],
        out_specs=row(0),
        scratch_shapes=[pltpu.VMEM((nr, width), BF16), pltpu.VMEM((nr, 1), F32),
                        pltpu.VMEM((nr, 1), F32), pltpu.VMEM((nr, width), F32)],
    )
    return pl.pallas_call(
        functools.partial(_decode_attn_kernel, pps=pps, lam_init=lam_init),
        grid_spec=grid_spec,
        out_shape=jax.ShapeDtypeStruct((B, 1, width), BF16),
        compiler_params=_params("parallel", "arbitrary"),
        name=name,
    )(page_table.reshape(-1), p, p, k_new, v_new, sb, lam_rows, subln_g.reshape(1, B_HD2),
      *([cache_k] * pps), *([cache_v] * pps))


def _run_group(x3, mem_k, mem_v, past, W, *, prompt):
    B, L, D = x3.shape
    M = B * L
    depth = W["norm_g"].shape[0]
    n_a = W["w_in_a"].shape[0]
    e_a = W["sgu_ln_g"].shape[1]
    b_width = W["w_k"].shape[1]
    mem_width = mem_k.shape[-1]
    H = b_width // B_HD2
    tag = "p" if prompt else "s"
    tm, tn = (512, 1024) if prompt else (M, 1024)
    x = x3.reshape(M, D)
    chunk_vs = []
    k_new = v_new = k_att = None
    for l in range(depth):
        if l < n_a:
            p = rms_matmul(x, W["norm_g"][l], W["w_in_a"][l], tm=tm, tn=tn, name=f"in_a{l}_{tag}")
            if prompt:
                main = sgu_prompt(p, W["sgu_ln_g"][l], W["sgu_ln_b"][l], W["sgu_w"][l], W["sgu_b"][l],
                                  width=e_a, rows=2 * CHUNK, name=f"sgu{l}_{tag}")
            else:
                main, vn = sgu_first_row(p, W["sgu_ln_g"][l], W["sgu_ln_b"][l], W["sgu_w"][l], W["sgu_b"][l],
                                         width=e_a, name=f"sgu{l}_{tag}")
                chunk_vs.append(vn)
            q_block = 3 * e_a // mem_width
            w_out = W["w_out_a"][l]
        else:
            jb = l - n_a
            p = rms_matmul(x, W["norm_g"][l], W["w_in_b"][jb], tm=tm, tn=tn, name=f"in_b{jb}_{tag}")
            lam_init = 0.8 - 0.6 * math.exp(-0.3 * l)
            lam_rows = jnp.stack([W["lambda_q1"][jb], W["lambda_k1"][jb], W["lambda_q2"][jb], W["lambda_k2"][jb]])
            if prompt:
                main = diff_attention_prompt(p.reshape(B, L, -1), k_att, v_new, W["rel_bias"], lam_rows,
                                             W["subln_g"][jb], lam_init=lam_init, t=512, name=f"dattn{jb}_{tag}")
            else:
                main = diff_attention_decode(p.reshape(B, L, -1), k_new.reshape(B, L, -1), v_new.reshape(B, L, -1),
                                             *past, W["rel_bias"], lam_rows, W["subln_g"][jb],
                                             lam_init=lam_init, pps=4, name=f"dattn{jb}_{tag}")
            main = main.reshape(M, b_width)
            q_block = 2 * b_width // mem_width
            w_out = W["w_out_b"][jb]
        mem_o = mem_attention(p.reshape(B, L, -1), mem_k[l], mem_v[l], q_block=q_block, tm=512,
                              name=f"mem{l}_{tag}")
        x = out_proj(main, mem_o.reshape(M, mem_width), w_out, x, tm=tm, tn=tn, name=f"out{l}_{tag}")
        if l == n_a - 1 and prompt:
            k_new, k_att = k_proj_prompt(x, W["kv_norm_g"], W["w_k"], B=B, tm=tm, heads=2, name=f"k_{tag}")
            v_new = v_proj_prompt(x, W["kv_norm_g"], W["w_v"], B=B, tm=tm, heads=4, name=f"v_{tag}")
        elif l == n_a - 1:
            k_new = rms_matmul(x, W["kv_norm_g"], W["w_k"], tm=tm, tn=tn, name=f"k_{tag}")
            v_new = rms_matmul(x, W["kv_norm_g"], W["w_v"], tm=tm, tn=tn, name=f"v_{tag}")
    y = rmsnorm_rows(x, W["final_norm_g"], tm=256, name=f"final_{tag}")
    if prompt:
        v_new = jnp.transpose(v_new, (0, 2, 1, 3))
    return (y.reshape(B, L, D), k_new.reshape(B, L, H, 2, B_HD), v_new.reshape(B, L, H, B_HD2), chunk_vs)


def kernel(x_prompt, x_sample, cache_k, cache_v, cache_mem_k, cache_mem_v, page_table, mem_prompt,
           norm_g, final_norm_g, mem_norm_g, w_mem_kv, w_in_a, sgu_ln_g, sgu_ln_b, sgu_w, sgu_b,
           w_out_a, kv_norm_g, w_kv, w_in_b, lambda_q1, lambda_k1, lambda_q2, lambda_k2,
           subln_g, rel_bias, w_out_b):
    depth = norm_g.shape[0]
    b_width = w_kv.shape[1] // 2
    mem_width = w_mem_kv.shape[-1] // 2
    mem_hd = mem_width // MEM_HEADS
    W = dict(norm_g=norm_g, final_norm_g=final_norm_g, sgu_ln_g=sgu_ln_g, sgu_ln_b=sgu_ln_b, sgu_w=sgu_w,
             sgu_b=sgu_b, kv_norm_g=kv_norm_g, lambda_q1=lambda_q1, lambda_k1=lambda_k1, lambda_q2=lambda_q2,
             lambda_k2=lambda_k2, subln_g=subln_g, rel_bias=rel_bias,
             w_in_a=w_in_a.astype(BF16), w_out_a=w_out_a.astype(BF16), w_in_b=w_in_b.astype(BF16),
             w_out_b=w_out_b.astype(BF16), w_k=w_kv[:, :b_width].astype(BF16), w_v=w_kv[:, b_width:].astype(BF16))

    bp, m_len, d_model = mem_prompt.shape
    mkv = rms_matmul_stacked(mem_prompt.reshape(bp * m_len, d_model), mem_norm_g, w_mem_kv.astype(BF16),
                             tm=512, tn=mem_width, name="mem_kv")
    mkv = mkv.reshape(depth, 2, bp, m_len, mem_width)
    mem_k_p, mem_v_p = mkv[:, 0], mkv[:, 1]
    y_p, k_p, v_p, _ = _run_group(x_prompt, mem_k_p, mem_v_p, None, W, prompt=True)

    bd = x_sample.shape[0]
    past = (cache_k.reshape(-1, B_HD), jnp.transpose(cache_v, (0, 2, 1, 3)), page_table)
    smk = cache_mem_k.reshape(depth, bd, -1, mem_width)
    smv = cache_mem_v.reshape(depth, bd, -1, mem_width)
    y_s, k_s, v_s, chunk_vs = _run_group(x_sample, smk, smv, past, W, prompt=False)
    chunk_v_s = jnp.stack(chunk_vs).reshape(len(chunk_vs), bd, x_sample.shape[1], -1)

    mem_shape = (depth, bp, m_len, MEM_HEADS, mem_hd)
    return (y_p, y_s, k_p, v_p, k_s, v_s, mem_k_p.reshape(mem_shape), mem_v_p.reshape(mem_shape), chunk_v_s)
```

```python
import functools
import math

import jax
import jax.numpy as jnp
from jax import lax
from jax.experimental import pallas as pl
from jax.experimental.pallas import tpu as pltpu

F32 = jnp.float32
BF16 = jnp.bfloat16

CHUNK = 128
SGU_GROUPS = 8
B_HD2 = 256
B_HD = B_HD2 // 2
MEM_HEADS = 4
N_BUCKETS = 32
MAX_DISTANCE = 128
PAGE_SIZE = 128
EPS = 1e-6
NEG = -1e30

VMEM_BYTES_V7X = 64 * 1024 * 1024
VMEM_LIMIT = VMEM_BYTES_V7X - 8 * 1024 * 1024

NT_DIMS = (((1,), (1,)), ((), ()))


def _params(*sem):
    return pltpu.CompilerParams(dimension_semantics=sem, vmem_limit_bytes=VMEM_LIMIT)


def _normalised_rows(x_ref, g_ref, h_ref):
    @pl.when(pl.program_id(1) == 0)
    def _():
        x = x_ref[...]
        ms = jnp.mean(x * x, axis=-1, keepdims=True)
        h_ref[...] = (x * lax.rsqrt(ms + EPS) * g_ref[...]).astype(BF16)

    return h_ref[...]


def _rms_matmul_kernel(x_ref, g_ref, w_ref, o_ref, h_ref):
    h = _normalised_rows(x_ref, g_ref, h_ref)
    o_ref[...] = jnp.dot(h, w_ref[...], preferred_element_type=F32).astype(o_ref.dtype)


def rms_matmul(x, g, w, *, tm, tn, name):
    M, D = x.shape
    N = w.shape[1]
    return pl.pallas_call(
        _rms_matmul_kernel,
        grid=(M // tm, N // tn),
        in_specs=[pl.BlockSpec((tm, D), lambda i, j: (i, 0)),
                  pl.BlockSpec((1, D), lambda i, j: (0, 0)),
                  pl.BlockSpec((D, tn), lambda i, j: (0, j))],
        out_specs=pl.BlockSpec((tm, tn), lambda i, j: (i, j)),
        out_shape=jax.ShapeDtypeStruct((M, N), F32),
        scratch_shapes=[pltpu.VMEM((tm, D), BF16)],
        compiler_params=_params("parallel", "arbitrary"),
        name=name,
    )(x, g.reshape(1, D), w)


def _rms_matmul_cast_kernel(x_ref, g_ref, w_ref, o_ref, wb_ref, h_ref):
    h = _normalised_rows(x_ref, g_ref, h_ref)
    wb = w_ref[...].astype(BF16)
    wb_ref[...] = wb
    o_ref[...] = jnp.dot(h, wb, preferred_element_type=F32)


def _f32_weight_spec(w, layer, tn, first_block=0):
    if w.ndim == 2:
        return pl.BlockSpec((w.shape[0], tn), lambda i, j: (0, first_block + j))
    return pl.BlockSpec((None, w.shape[1], tn), lambda i, j: (layer, 0, first_block + j))


def rms_matmul_cast(x, g, w, *, tn, name, layer=0, col0=0, n_cols=None):
    M, D = x.shape
    n_cols = w.shape[-1] if n_cols is None else n_cols
    return pl.pallas_call(
        _rms_matmul_cast_kernel,
        grid=(1, n_cols // tn),
        in_specs=[pl.BlockSpec((M, D), lambda i, j: (0, 0)),
                  pl.BlockSpec((1, D), lambda i, j: (0, 0)),
                  _f32_weight_spec(w, layer, tn, col0 // tn)],
        out_specs=[pl.BlockSpec((M, tn), lambda i, j: (0, j)),
                   pl.BlockSpec((D, tn), lambda i, j: (0, j))],
        out_shape=[jax.ShapeDtypeStruct((M, n_cols), F32),
                   jax.ShapeDtypeStruct((D, n_cols), BF16)],
        scratch_shapes=[pltpu.VMEM((M, D), BF16)],
        compiler_params=_params("arbitrary", "arbitrary"),
        name=name,
    )(x, g.reshape(1, D), w)


def rms_matmul_stacked(x, g, w, *, tm, tn, name):
    M, D = x.shape
    G, _, N = w.shape
    nb = N // tn
    return pl.pallas_call(
        _rms_matmul_kernel,
        grid=(M // tm, G * nb),
        in_specs=[pl.BlockSpec((tm, D), lambda i, j: (i, 0)),
                  pl.BlockSpec((1, D), lambda i, j: (0, 0)),
                  pl.BlockSpec((None, D, tn), lambda i, j: (j // nb, 0, j % nb))],
        out_specs=pl.BlockSpec((None, tm, tn), lambda i, j: (j, i, 0)),
        out_shape=jax.ShapeDtypeStruct((G * nb, M, tn), F32),
        scratch_shapes=[pltpu.VMEM((tm, D), BF16)],
        compiler_params=_params("parallel", "arbitrary"),
        name=name,
    )(x, g.reshape(1, D), w)


def _k_proj_kernel(x_ref, g_ref, w_ref, knat_ref, katt_ref, h_ref):
    tm = x_ref.shape[0]
    heads = katt_ref.shape[0]
    j = pl.program_id(1)
    h = _normalised_rows(x_ref, g_ref, h_ref)
    res = jnp.dot(h, w_ref[...], preferred_element_type=F32)
    n_sub = knat_ref.shape[0] // tm
    for q in range(2 * heads):
        knat_ref[pl.ds(j * 2 * heads + q, tm, stride=n_sub), :] = res[:, q * B_HD:(q + 1) * B_HD]
    for hh in range(heads):
        katt_ref[hh] = res[:, hh * B_HD2:(hh + 1) * B_HD2].astype(BF16)


def k_proj_prompt(x, g, w, *, B, tm, heads, name):
    M, D = x.shape
    L = M // B
    N = w.shape[1]
    H = N // B_HD2
    nlb = L // tm
    return pl.pallas_call(
        _k_proj_kernel,
        grid=(M // tm, H // heads),
        in_specs=[pl.BlockSpec((tm, D), lambda i, j: (i, 0)),
                  pl.BlockSpec((1, D), lambda i, j: (0, 0)),
                  pl.BlockSpec((D, heads * B_HD2), lambda i, j: (0, j))],
        out_specs=[pl.BlockSpec((tm * 2 * H, B_HD), lambda i, j: (i, 0)),
                   pl.BlockSpec((None, heads, tm, B_HD2), lambda i, j: (i // nlb, j, i % nlb, 0))],
        out_shape=[jax.ShapeDtypeStruct((M * 2 * H, B_HD), F32),
                   jax.ShapeDtypeStruct((B, H, L, B_HD2), BF16)],
        scratch_shapes=[pltpu.VMEM((tm, D), BF16)],
        compiler_params=_params("parallel", "arbitrary"),
        name=name,
    )(x, g.reshape(1, D), w)


def _v_proj_kernel(x_ref, g_ref, w_ref, v_ref, h_ref):
    h = _normalised_rows(x_ref, g_ref, h_ref)
    res = jnp.dot(h, w_ref[...], preferred_element_type=F32)
    for hh in range(v_ref.shape[0]):
        v_ref[hh] = res[:, hh * B_HD2:(hh + 1) * B_HD2]


def v_proj_prompt(x, g, w, *, B, tm, heads, name):
    M, D = x.shape
    L = M // B
    H = w.shape[1] // B_HD2
    nlb = L // tm
    return pl.pallas_call(
        _v_proj_kernel,
        grid=(M // tm, H // heads),
        in_specs=[pl.BlockSpec((tm, D), lambda i, j: (i, 0)),
                  pl.BlockSpec((1, D), lambda i, j: (0, 0)),
                  pl.BlockSpec((D, heads * B_HD2), lambda i, j: (0, j))],
        out_specs=pl.BlockSpec((None, heads, tm, B_HD2), lambda i, j: (i // nlb, j, i % nlb, 0)),
        out_shape=jax.ShapeDtypeStruct((B, H, L, B_HD2), F32),
        scratch_shapes=[pltpu.VMEM((tm, D), BF16)],
        compiler_params=_params("parallel", "arbitrary"),
        name=name,
    )(x, g.reshape(1, D), w)


def _out_proj_kernel(a_ref, m_ref, w_ref, x_ref, o_ref):
    ka = a_ref.shape[1]
    acc = jnp.dot(a_ref[...], w_ref[:ka, :], preferred_element_type=F32)
    acc = acc + jnp.dot(m_ref[...], w_ref[ka:, :], preferred_element_type=F32)
    o_ref[...] = x_ref[...] + acc


def out_proj(a, m, w, x, *, tm, tn, name):
    M, ka = a.shape
    km = m.shape[1]
    N = w.shape[1]
    return pl.pallas_call(
        _out_proj_kernel,
        grid=(M // tm, N // tn),
        in_specs=[pl.BlockSpec((tm, ka), lambda i, j: (i, 0)),
                  pl.BlockSpec((tm, km), lambda i, j: (i, 0)),
                  pl.BlockSpec((ka + km, tn), lambda i, j: (0, j)),
                  pl.BlockSpec((tm, tn), lambda i, j: (i, j))],
        out_specs=pl.BlockSpec((tm, tn), lambda i, j: (i, j)),
        out_shape=jax.ShapeDtypeStruct((M, N), F32),
        compiler_params=_params("parallel", "arbitrary"),
        name=name,
    )(a, m, w, x)


def _out_proj_cast_kernel(a_ref, m_ref, w_ref, x_ref, o_ref, wb_ref):
    ka = a_ref.shape[1]
    wb = w_ref[...].astype(BF16)
    wb_ref[...] = wb
    acc = jnp.dot(a_ref[...], wb[:ka], preferred_element_type=F32)
    acc = acc + jnp.dot(m_ref[...], wb[ka:], preferred_element_type=F32)
    o_ref[...] = x_ref[...] + acc


def out_proj_cast(a, m, w, x, *, tn, name, layer=0):
    M, ka = a.shape
    km = m.shape[1]
    N = w.shape[-1]
    return pl.pallas_call(
        _out_proj_cast_kernel,
        grid=(1, N // tn),
        in_specs=[pl.BlockSpec((M, ka), lambda i, j: (0, 0)),
                  pl.BlockSpec((M, km), lambda i, j: (0, 0)),
                  _f32_weight_spec(w, layer, tn),
                  pl.BlockSpec((M, tn), lambda i, j: (0, j))],
        out_specs=[pl.BlockSpec((M, tn), lambda i, j: (0, j)),
                   pl.BlockSpec((ka + km, tn), lambda i, j: (0, j))],
        out_shape=[jax.ShapeDtypeStruct((M, N), F32),
                   jax.ShapeDtypeStruct((ka + km, N), BF16)],
        compiler_params=_params("arbitrary", "arbitrary"),
        name=name,
    )(a, m, w, x)


def _out_proj_norm_kernel(a_ref, m_ref, w_ref, x_ref, g_ref, o_ref):
    j = pl.program_id(1)
    ka = a_ref.shape[1]
    tn = x_ref.shape[1]
    acc = jnp.dot(a_ref[...], w_ref[:ka, :], preferred_element_type=F32)
    acc = acc + jnp.dot(m_ref[...], w_ref[ka:, :], preferred_element_type=F32)
    o_ref[:, pl.ds(pl.multiple_of(j * tn, tn), tn)] = x_ref[...] + acc

    @pl.when(j == pl.num_programs(1) - 1)
    def _():
        y = o_ref[...]
        ms = jnp.mean(y * y, axis=-1, keepdims=True)
        o_ref[...] = y * lax.rsqrt(ms + EPS) * g_ref[...]


def out_proj_norm(a, m, w, x, g, *, tm, tn, name):
    M, ka = a.shape
    km = m.shape[1]
    N = w.shape[-1]
    return pl.pallas_call(
        _out_proj_norm_kernel,
        grid=(M // tm, N // tn),
        in_specs=[pl.BlockSpec((tm, ka), lambda i, j: (i, 0)),
                  pl.BlockSpec((tm, km), lambda i, j: (i, 0)),
                  pl.BlockSpec((ka + km, tn), lambda i, j: (0, j)),
                  pl.BlockSpec((tm, tn), lambda i, j: (i, j)),
                  pl.BlockSpec((1, N), lambda i, j: (0, 0))],
        out_specs=pl.BlockSpec((tm, N), lambda i, j: (i, 0)),
        out_shape=jax.ShapeDtypeStruct((M, N), F32),
        compiler_params=_params("parallel", "arbitrary"),
        name=name,
    )(a, m, w, x, g.reshape(1, N))


def _rmsnorm_kernel(x_ref, g_ref, o_ref):
    x = x_ref[...]
    ms = jnp.mean(x * x, axis=-1, keepdims=True)
    o_ref[...] = x * lax.rsqrt(ms + EPS) * g_ref[...]


def rmsnorm_rows(x, g, *, tm, name):
    M, D = x.shape
    tm = min(tm, M)
    return pl.pallas_call(
        _rmsnorm_kernel,
        grid=(M // tm,),
        in_specs=[pl.BlockSpec((tm, D), lambda i: (i, 0)),
                  pl.BlockSpec((1, D), lambda i: (0, 0))],
        out_specs=pl.BlockSpec((tm, D), lambda i: (i, 0)),
        out_shape=jax.ShapeDtypeStruct((M, D), F32),
        compiler_params=_params("parallel"),
        name=name,
    )(x, g.reshape(1, D))


def _layernorm(v, g, b):
    mu = jnp.mean(v, axis=-1, keepdims=True)
    vc = v - mu
    var = jnp.mean(vc * vc, axis=-1, keepdims=True)
    return vc * lax.rsqrt(var + EPS) * g + b


def _sgu_kernel(u_ref, v_ref, z_ref, lng_ref, lnb_ref, w_ref, bt_ref, o_ref):
    rows, width = u_ref.shape
    ch = width // SGU_GROUPS
    r = lax.broadcasted_iota(jnp.int32, (CHUNK, CHUNK), 0)
    c = lax.broadcasted_iota(jnp.int32, (CHUNK, CHUNK), 1)
    causal = c <= r
    ws = [jnp.where(causal, w_ref[g], 0.0).astype(BF16) for g in range(SGU_GROUPS)]
    for n in range(rows // CHUNK):
        rs = slice(n * CHUNK, (n + 1) * CHUNK)
        vn = _layernorm(jax.nn.gelu(v_ref[rs, :]), lng_ref[...], lnb_ref[...]).astype(BF16)
        for g in range(SGU_GROUPS):
            cs = slice(g * ch, (g + 1) * ch)
            mixed = jnp.dot(ws[g], vn[:, cs], preferred_element_type=F32) + bt_ref[:, g:g + 1]
            u = jax.nn.gelu(u_ref[rs, cs])
            o_ref[rs, cs] = (u * mixed * jax.nn.silu(z_ref[rs, cs])).astype(o_ref.dtype)


def sgu_prompt(p, ln_g, ln_b, w_s, b_s, *, width, rows, name):
    M = p.shape[0]
    return pl.pallas_call(
        _sgu_kernel,
        grid=(M // rows,),
        in_specs=[pl.BlockSpec((rows, width), lambda i: (i, 0)),
                  pl.BlockSpec((rows, width), lambda i: (i, 1)),
                  pl.BlockSpec((rows, width), lambda i: (i, 2)),
                  pl.BlockSpec((1, width), lambda i: (0, 0)),
                  pl.BlockSpec((1, width), lambda i: (0, 0)),
                  pl.BlockSpec((SGU_GROUPS, CHUNK, CHUNK), lambda i: (0, 0, 0)),
                  pl.BlockSpec((CHUNK, SGU_GROUPS), lambda i: (0, 0))],
        out_specs=pl.BlockSpec((rows, width), lambda i: (i, 0)),
        out_shape=jax.ShapeDtypeStruct((M, width), BF16),
        compiler_params=_params("parallel"),
        name=name,
    )(p, p, p, ln_g.reshape(1, width), ln_b.reshape(1, width), w_s, b_s.T)


def _sgu_first_row_kernel(u_ref, v_ref, z_ref, lng_ref, lnb_ref, w0_ref, b0_ref, o_ref, vn_ref):
    vn = _layernorm(jax.nn.gelu(v_ref[...]), lng_ref[...], lnb_ref[...])
    vn_ref[...] = vn
    mixed = w0_ref[...] * vn + b0_ref[...]
    o_ref[...] = (jax.nn.gelu(u_ref[...]) * mixed * jax.nn.silu(z_ref[...])).astype(o_ref.dtype)


def sgu_first_row(p, ln_g, ln_b, w_s, b_s, *, width, name):
    M = p.shape[0]
    ch = width // SGU_GROUPS
    w0 = jnp.repeat(w_s[:, 0, 0], ch).reshape(1, width)
    b0 = jnp.repeat(b_s[:, 0], ch).reshape(1, width)
    row = pl.BlockSpec((1, width), lambda i: (0, 0))
    return pl.pallas_call(
        _sgu_first_row_kernel,
        grid=(1,),
        in_specs=[pl.BlockSpec((M, width), lambda i: (0, 0)),
                  pl.BlockSpec((M, width), lambda i: (0, 1)),
                  pl.BlockSpec((M, width), lambda i: (0, 2)),
                  row, row, row, row],
        out_specs=[pl.BlockSpec((M, width), lambda i: (0, 0)),
                   pl.BlockSpec((M, width), lambda i: (0, 0))],
        out_shape=[jax.ShapeDtypeStruct((M, width), BF16),
                   jax.ShapeDtypeStruct((M, width), F32)],
        compiler_params=_params("arbitrary"),
        name=name,
    )(p, p, p, ln_g.reshape(1, width), ln_b.reshape(1, width), w0, b0)


def _mem_attn_kernel(q_ref, zm_ref, k_ref, v_ref, o_ref):
    tm, width = q_ref.shape
    hd = width // MEM_HEADS
    q = q_ref[...]
    if tm < 8:
        q = jnp.broadcast_to(q[0:1], (8, width))
    for h in range(MEM_HEADS):
        cs = slice(h * hd, (h + 1) * hd)
        s = lax.dot_general(q[:, cs].astype(BF16), k_ref[:, cs].astype(BF16), NT_DIMS,
                            preferred_element_type=F32) * (hd ** -0.5)
        e = jnp.exp(s - jnp.max(s, axis=-1, keepdims=True))
        p = e * (1.0 / jnp.sum(e, axis=-1, keepdims=True))
        o = jnp.dot(p.astype(BF16), v_ref[:, cs].astype(BF16), preferred_element_type=F32)
        o_ref[:, cs] = (o[:tm] * jax.nn.silu(zm_ref[:, cs])).astype(o_ref.dtype)


def mem_attention(p, mem_k, mem_v, *, lead_k, lead_v, q_block, tm, name):
    B, L, _ = p.shape
    mlen, width = mem_k.shape[-2:]
    tm = min(tm, L)
    mem_spec = lambda lead: pl.BlockSpec((None,) * (len(lead) + 1) + (mlen, width),
                                         lambda b, i: (*lead, b, 0, 0))
    return pl.pallas_call(
        _mem_attn_kernel,
        grid=(B, L // tm),
        in_specs=[pl.BlockSpec((None, tm, width), lambda b, i: (b, i, q_block)),
                  pl.BlockSpec((None, tm, width), lambda b, i: (b, i, q_block + 1)),
                  mem_spec(lead_k), mem_spec(lead_v)],
        out_specs=pl.BlockSpec((None, tm, width), lambda b, i: (b, i, 0)),
        out_shape=jax.ShapeDtypeStruct((B, L, width), BF16),
        compiler_params=_params("parallel", "parallel"),
        name=name,
    )(p, p, mem_k, mem_v)


def _rel_bucket(dist):
    n = jnp.maximum(dist, 0)
    max_exact = N_BUCKETS // 2
    nf = jnp.maximum(n, 1).astype(F32)
    large = max_exact + (jnp.log(nf / max_exact) / math.log(MAX_DISTANCE / max_exact)
                         * (N_BUCKETS - max_exact)).astype(jnp.int32)
    large = jnp.minimum(large, N_BUCKETS - 1)
    return jnp.where(n < max_exact, n, large)


def _lambda(lam_ref, lam_init):
    a = jnp.sum(lam_ref[0:1, :] * lam_ref[1:2, :], axis=-1, keepdims=True)
    b = jnp.sum(lam_ref[2:3, :] * lam_ref[3:4, :], axis=-1, keepdims=True)
    return jnp.exp(a) - jnp.exp(b) + lam_init


def _subln_gate(o, g, z, lam_init):
    ms = jnp.mean(o * o, axis=-1, keepdims=True)
    on = (o * lax.rsqrt(ms + EPS) * g) * (1.0 - lam_init)
    return on * jax.nn.silu(z)


def _lane_groups(x):
    return [x[:, g * 128:(g + 1) * 128] for g in range(x.shape[1] // 128)]


def _diff_attn_kernel(q_ref, z_ref, k_ref, v_ref, bd_ref, be_ref, bfar_ref, lam_ref, g_ref, o_ref,
                      vb_ref, b0_ref, b1_ref, s_ref, mp_ref, lp_ref, acc_ref, *, lam_init):
    t = q_ref.shape[0]
    qi = pl.program_id(2)
    scale = B_HD ** -0.5
    md = MAX_DISTANCE
    nb = t // md

    @pl.when(qi == 0)
    def _():
        vb_ref[...] = v_ref[...].astype(BF16)
        far = jnp.broadcast_to(bfar_ref[0:1, 0:1], (md, md))
        masked = jnp.full((md, md), NEG, F32)
        for bi in range(nb):
            for bj in range(nb):
                piece = bd_ref[...] if bi == bj else be_ref[...] if bi == bj + 1 else far if bi > bj else masked
                b0_ref[bi * md:(bi + 1) * md, bj * md:(bj + 1) * md] = piece
                piece = be_ref[...] if (bi == 0 and bj == nb - 1) else far
                b1_ref[bi * md:(bi + 1) * md, bj * md:(bj + 1) * md] = piece

    q = q_ref[...]
    qs = (q[:, :B_HD].astype(BF16), q[:, B_HD:].astype(BF16))
    mp_ref[...] = jnp.full(mp_ref.shape, NEG, F32)
    lp_ref[...] = jnp.zeros(lp_ref.shape, F32)
    acc_ref[...] = jnp.zeros(acc_ref.shape, F32)

    def scores(kj, bias):
        off = pl.multiple_of(kj * t, t)
        k = k_ref[pl.ds(off, t), :]
        for c in range(2):
            s = lax.dot_general(qs[c], k[:, c * B_HD:(c + 1) * B_HD], NT_DIMS,
                                preferred_element_type=F32) * scale + bias
            s_ref[c, :, pl.ds(off, t)] = s
            mp_ref[c] = functools.reduce(jnp.maximum, _lane_groups(s), mp_ref[c])

    bfar = bfar_ref[0:1, 0:1]

    def far(kj, carry):
        scores(kj, bfar)
        return carry

    lax.fori_loop(0, jnp.maximum(qi - 1, 0), far, 0)

    @pl.when(qi >= 1)
    def _():
        scores(qi - 1, b1_ref[...])

    scores(qi, b0_ref[...])

    m_rows = [jnp.broadcast_to(jnp.max(mp_ref[c], axis=-1, keepdims=True), (t, 128)) for c in range(2)]

    def accumulate(kj, carry):
        off = pl.multiple_of(kj * t, t)
        v = vb_ref[pl.ds(off, t), :]
        for c in range(2):
            ps = [jnp.exp(sg - m_rows[c]) for sg in _lane_groups(s_ref[c, :, pl.ds(off, t)])]
            lp_ref[c] = functools.reduce(jnp.add, ps, lp_ref[c])
            p = jnp.concatenate(ps, axis=1).astype(BF16)
            acc_ref[c] = acc_ref[c] + jnp.dot(p, v, preferred_element_type=F32)
        return carry

    lax.fori_loop(0, qi + 1, accumulate, 0)

    lam = _lambda(lam_ref, lam_init)
    inv_l = [1.0 / jnp.sum(lp_ref[c], axis=-1, keepdims=True) for c in range(2)]
    o = acc_ref[0] * inv_l[0] - lam * (acc_ref[1] * inv_l[1])
    o_ref[...] = _subln_gate(o, g_ref[...], z_ref[...], lam_init).astype(o_ref.dtype)


def _toeplitz_tiles(g, t):
    H, P = g.shape
    hankel = jnp.tile(g[:, ::-1], (1, t + 1))[:, :t * (P + 1)].reshape(H, t, P + 1)
    return hankel[:, ::-1, :t]


def diff_attention_prompt(p, k, v, rel_bias, lam_rows, subln_g, *, lam_init, t, name):
    B, L, _ = p.shape
    H = k.shape[1]
    md = MAX_DISTANCE
    assert t % md == 0 and L % t == 0
    table = rel_bias.astype(F32)
    d = jnp.arange(-(md - 1), md, dtype=jnp.int32)
    bd = _toeplitz_tiles(jnp.where((d >= 0)[None], table[_rel_bucket(d)].T, NEG), md)
    be = _toeplitz_tiles(table[_rel_bucket(d + md)].T, md)
    bfar = jnp.broadcast_to(table[_rel_bucket(jnp.int32(md + 1))][:, None, None], (H, 1, 128))
    blk = lambda off: pl.BlockSpec((None, t, B_HD2), lambda b, h, i: (b, i, off + h))
    kv = pl.BlockSpec((None, None, L, B_HD2), lambda b, h, i: (b, h, 0, 0))
    tile = pl.BlockSpec((None, md, md), lambda b, h, i: (h, 0, 0))
    return pl.pallas_call(
        functools.partial(_diff_attn_kernel, lam_init=lam_init),
        grid=(B, H, L // t),
        in_specs=[blk(0), blk(H), kv, kv, tile, tile,
                  pl.BlockSpec((None, 1, 128), lambda b, h, i: (h, 0, 0)),
                  pl.BlockSpec((4, B_HD), lambda b, h, i: (0, 0)),
                  pl.BlockSpec((1, B_HD2), lambda b, h, i: (0, 0))],
        out_specs=pl.BlockSpec((None, t, B_HD2), lambda b, h, i: (b, i, h)),
        out_shape=jax.ShapeDtypeStruct((B, L, H * B_HD2), BF16),
        scratch_shapes=[pltpu.VMEM((L, B_HD2), BF16), pltpu.VMEM((t, t), F32), pltpu.VMEM((t, t), F32),
                        pltpu.VMEM((2, t, L), F32), pltpu.VMEM((2, t, 128), F32), pltpu.VMEM((2, t, 128), F32),
                        pltpu.VMEM((2, t, B_HD2), F32)],
        compiler_params=_params("parallel", "parallel", "arbitrary"),
        name=name,
    )(p, p, k, v, bd, be, bfar, lam_rows, subln_g.reshape(1, B_HD2))


def _decode_attn_kernel(pt_ref, q_ref, z_ref, kn_ref, vn_ref, sb_ref, lam_ref, g_ref, *rest,
                        pps, lam_init):
    del pt_ref
    k_pages, v_pages = rest[:pps], rest[pps:2 * pps]
    o_ref, qb_ref, m_ref, l_ref, acc_ref = rest[2 * pps:]
    nr, width = qb_ref.shape
    H = width // B_HD2
    j = pl.program_id(1)
    last = pl.num_programs(1) - 1
    scale = B_HD ** -0.5
    r_id = lax.broadcasted_iota(jnp.int32, (nr, width), 0)
    c_id = lax.broadcasted_iota(jnp.int32, (nr, width), 1)
    own = (c_id >> 7) == r_id
    q_rows = jnp.where(own, q_ref[...], 0.0)

    @pl.when(j == 0)
    def _():
        qb_ref[...] = q_rows.astype(BF16)
        m_ref[...] = jnp.full(m_ref.shape, NEG, F32)
        l_ref[...] = jnp.zeros(l_ref.shape, F32)
        acc_ref[...] = jnp.zeros(acc_ref.shape, F32)

    def update(s, pv_fn):
        m_old = m_ref[...]
        m_new = jnp.maximum(m_old, jnp.max(s, axis=-1, keepdims=True))
        alpha = jnp.exp(m_old - m_new)
        p = jnp.exp(s - m_new)
        l_ref[...] = alpha * l_ref[...] + jnp.sum(p, axis=-1, keepdims=True)
        acc_ref[...] = alpha * acc_ref[...] + pv_fn(p)
        m_ref[...] = m_new

    for r in range(pps):
        bias = sb_ref[0]
        if r == pps - 1:
            bias = jnp.where(j == last, sb_ref[1], bias)
        kt = jnp.concatenate([k_pages[r][pl.ds(hc, PAGE_SIZE, stride=2 * H), :] for hc in range(2 * H)],
                             axis=1).astype(BF16)
        vb = jnp.concatenate([v_pages[r][h] for h in range(H)], axis=1).astype(BF16)
        s = lax.dot_general(qb_ref[...], kt, NT_DIMS, preferred_element_type=F32) * scale + bias
        update(s, lambda p, vb=vb: jnp.dot(p.astype(BF16), vb, preferred_element_type=F32))

    @pl.when(j == last)
    def _():
        s = jnp.sum(q_rows * kn_ref[...], axis=-1, keepdims=True) * scale + sb_ref[2][:, 0:1]
        vn = vn_ref[...]
        update(s, lambda p: p * vn)
        lam = _lambda(lam_ref, lam_init)
        o_all = acc_ref[...] * (1.0 / l_ref[...])
        for h in range(H):
            cs = slice(h * B_HD2, (h + 1) * B_HD2)
            o = o_all[2 * h:2 * h + 1, cs] - lam * o_all[2 * h + 1:2 * h + 2, cs]
            o_ref[:, cs] = _subln_gate(o, g_ref[...], z_ref[:, cs], lam_init).astype(o_ref.dtype)


def diff_attention_decode(p, k_new, v_new, cache_k, cache_v, page_table, rel_bias, lam_rows, subln_g,
                          *, lam_init, pps, name):
    B = p.shape[0]
    width = k_new.shape[-1]
    H = width // B_HD2
    n_pages = page_table.shape[1]
    k_rows = PAGE_SIZE * 2 * H
    assert n_pages % pps == 0 and PAGE_SIZE >= MAX_DISTANCE
    nr = -(-2 * H // 8) * 8
    table = rel_bias.astype(F32)
    per_row = lambda x: jnp.pad(jnp.repeat(x, 2, axis=0), ((0, nr - 2 * H), (0, 0)))
    dist_last = PAGE_SIZE - jnp.arange(PAGE_SIZE, dtype=jnp.int32)
    sb = jnp.stack([
        per_row(jnp.broadcast_to(table[_rel_bucket(jnp.int32(PAGE_SIZE + 1))][:, None], (H, PAGE_SIZE))),
        per_row(table[_rel_bucket(dist_last)].T),
        per_row(jnp.broadcast_to(table[_rel_bucket(jnp.int32(0))][:, None], (H, PAGE_SIZE))),
    ])
    row = lambda c: pl.BlockSpec((None, 1, width), lambda b, j, pt: (b, 0, c))
    k_page = lambda r: pl.BlockSpec((k_rows, B_HD), lambda b, j, pt: (pt[b * n_pages + j * pps + r], 0))
    v_page = lambda r: pl.BlockSpec((None, H, PAGE_SIZE, B_HD2),
                                    lambda b, j, pt: (pt[b * n_pages + j * pps + r], 0, 0, 0))
    grid_spec = pltpu.PrefetchScalarGridSpec(
        num_scalar_prefetch=1,
        grid=(B, n_pages // pps),
        in_specs=[row(0), row(1), row(0), row(0),
                  pl.BlockSpec((3, nr, PAGE_SIZE), lambda b, j, pt: (0, 0, 0)),
                  pl.BlockSpec((4, B_HD), lambda b, j, pt: (0, 0)),
                  pl.BlockSpec((1, B_HD2), lambda b, j, pt: (0, 0))]
                 + [k_page(r) for r in range(pps)] + [v_page(r) for r in range(pps)],
        out_specs=row(0),
        scratch_shapes=[pltpu.VMEM((nr, width), BF16), pltpu.VMEM((nr, 1), F32),
                        pltpu.VMEM((nr, 1), F32), pltpu.VMEM((nr, width), F32)],
    )
    return pl.pallas_call(
        functools.partial(_decode_attn_kernel, pps=pps, lam_init=lam_init),
        grid_spec=grid_spec,
        out_shape=jax.ShapeDtypeStruct((B, 1, width), BF16),
        compiler_params=_params("parallel", "arbitrary"),
        name=name,
    )(page_table.reshape(-1), p, p, k_new, v_new, sb, lam_rows, subln_g.reshape(1, B_HD2),
      *([cache_k] * pps), *([cache_v] * pps))


def _run_group(x3, mem_k, mem_v, lead_k, lead_v, past, W, Wb, *, prompt):
    B, L, D = x3.shape
    M = B * L
    depth = W["norm_g"].shape[0]
    n_a = W["w_in_a"].shape[0]
    e_a = W["sgu_ln_g"].shape[1]
    b_width = W["w_kv"].shape[1] // 2
    mem_width = mem_k.shape[-1]
    H = b_width // B_HD2
    tag = "p" if prompt else "s"
    tm, tn = 512, 1024
    tn_cast = 512
    x = x3.reshape(M, D)
    chunk_vs = []
    k_new = v_new = k_att = None

    def in_proj(key, x, g, w, layer, name, **cols):
        if prompt:
            return rms_matmul(x, g, Wb[key], tm=tm, tn=tn, name=name)
        out, Wb[key] = rms_matmul_cast(x, g, w, layer=layer, tn=tn_cast, name=name, **cols)
        return out

    for l in range(depth):
        if l < n_a:
            p = in_proj(("in", l), x, W["norm_g"][l], W["w_in_a"], l, f"in_a{l}_{tag}")
            if prompt:
                main = sgu_prompt(p, W["sgu_ln_g"][l], W["sgu_ln_b"][l], W["sgu_w"][l], W["sgu_b"][l],
                                  width=e_a, rows=2 * CHUNK, name=f"sgu{l}_{tag}")
            else:
                main, vn = sgu_first_row(p, W["sgu_ln_g"][l], W["sgu_ln_b"][l], W["sgu_w"][l], W["sgu_b"][l],
                                         width=e_a, name=f"sgu{l}_{tag}")
                chunk_vs.append(vn)
            q_block = 3 * e_a // mem_width
            w_out, w_layer = W["w_out_a"], l
        else:
            jb = l - n_a
            p = in_proj(("in", l), x, W["norm_g"][l], W["w_in_b"], jb, f"in_b{jb}_{tag}")
            lam_init = 0.8 - 0.6 * math.exp(-0.3 * l)
            lam_rows = jnp.stack([W["lambda_q1"][jb], W["lambda_k1"][jb], W["lambda_q2"][jb], W["lambda_k2"][jb]])
            if prompt:
                main = diff_attention_prompt(p.reshape(B, L, -1), k_att, v_new, W["rel_bias"], lam_rows,
                                             W["subln_g"][jb], lam_init=lam_init, t=512, name=f"dattn{jb}_{tag}")
            else:
                main = diff_attention_decode(p.reshape(B, L, -1), k_new.reshape(B, L, -1), v_new.reshape(B, L, -1),
                                             *past, W["rel_bias"], lam_rows, W["subln_g"][jb],
                                             lam_init=lam_init, pps=4, name=f"dattn{jb}_{tag}")
            main = main.reshape(M, b_width)
            q_block = 2 * b_width // mem_width
            w_out, w_layer = W["w_out_b"], jb
        mem_o = mem_attention(p.reshape(B, L, -1), mem_k, mem_v, lead_k=(l, *lead_k), lead_v=(l, *lead_v),
                              q_block=q_block, tm=512, name=f"mem{l}_{tag}")
        mem_o = mem_o.reshape(M, mem_width)
        if not prompt:
            x, Wb["out", l] = out_proj_cast(main, mem_o, w_out, x, layer=w_layer, tn=tn_cast, name=f"out{l}_{tag}")
        elif l < depth - 1:
            x = out_proj(main, mem_o, Wb["out", l], x, tm=tm, tn=tn, name=f"out{l}_{tag}")
        else:
            y = out_proj_norm(main, mem_o, Wb["out", l], x, W["final_norm_g"], tm=tm, tn=tn, name=f"out{l}_{tag}")
        if l == n_a - 1 and prompt:
            k_new, k_att = k_proj_prompt(x, W["kv_norm_g"], Wb["k"], B=B, tm=tm, heads=2, name=f"k_{tag}")
            v_new = v_proj_prompt(x, W["kv_norm_g"], Wb["v"], B=B, tm=tm, heads=4, name=f"v_{tag}")
        elif l == n_a - 1:
            k_new = in_proj("k", x, W["kv_norm_g"], W["w_kv"], 0, f"k_{tag}", col0=0, n_cols=b_width)
            v_new = in_proj("v", x, W["kv_norm_g"], W["w_kv"], 0, f"v_{tag}", col0=b_width, n_cols=b_width)
    if prompt:
        v_new = jnp.transpose(v_new, (0, 2, 1, 3))
    else:
        y = rmsnorm_rows(x, W["final_norm_g"], tm=M, name=f"final_{tag}")
    return (y.reshape(B, L, D), k_new.reshape(B, L, H, 2, B_HD), v_new.reshape(B, L, H, B_HD2), chunk_vs)


def kernel(x_prompt, x_sample, cache_k, cache_v, cache_mem_k, cache_mem_v, page_table, mem_prompt,
           norm_g, final_norm_g, mem_norm_g, w_mem_kv, w_in_a, sgu_ln_g, sgu_ln_b, sgu_w, sgu_b,
           w_out_a, kv_norm_g, w_kv, w_in_b, lambda_q1, lambda_k1, lambda_q2, lambda_k2,
           subln_g, rel_bias, w_out_b):
    depth = norm_g.shape[0]
    b_width = w_kv.shape[1] // 2
    mem_width = w_mem_kv.shape[-1] // 2
    mem_hd = mem_width // MEM_HEADS
    W = dict(norm_g=norm_g, final_norm_g=final_norm_g, sgu_ln_g=sgu_ln_g, sgu_ln_b=sgu_ln_b, sgu_w=sgu_w,
             sgu_b=sgu_b, kv_norm_g=kv_norm_g, lambda_q1=lambda_q1, lambda_k1=lambda_k1, lambda_q2=lambda_q2,
             lambda_k2=lambda_k2, subln_g=subln_g, rel_bias=rel_bias,
             w_in_a=w_in_a, w_out_a=w_out_a, w_in_b=w_in_b, w_out_b=w_out_b, w_kv=w_kv)

    bd = x_sample.shape[0]
    past = (cache_k.reshape(-1, B_HD), jnp.transpose(cache_v, (0, 2, 1, 3)), page_table)
    smk = cache_mem_k.reshape(depth, bd, -1, mem_width)
    smv = cache_mem_v.reshape(depth, bd, -1, mem_width)
    Wb = {}
    y_s, k_s, v_s, chunk_vs = _run_group(x_sample, smk, smv, (), (), past, W, Wb, prompt=False)
    chunk_v_s = jnp.stack(chunk_vs).reshape(len(chunk_vs), bd, x_sample.shape[1], -1)

    bp, m_len, d_model = mem_prompt.shape
    mkv = rms_matmul_stacked(mem_prompt.reshape(bp * m_len, d_model), mem_norm_g, w_mem_kv.astype(BF16),
                             tm=512, tn=mem_width, name="mem_kv")
    mkv = mkv.reshape(depth, 2, bp, m_len, mem_width)
    mem_k_p, mem_v_p = mkv[:, 0], mkv[:, 1]
    y_p, k_p, v_p, _ = _run_group(x_prompt, mkv, mkv, (0,), (1,), None, W, Wb, prompt=True)

    mem_shape = (depth, bp, m_len, MEM_HEADS, mem_hd)
    return (y_p, y_s, k_p, v_p, k_s, v_s, mem_k_p.reshape(mem_shape), mem_v_p.reshape(mem_shape), chunk_v_s)
```

```python
import functools
import math

import jax
import jax.numpy as jnp
from jax import lax
from jax.experimental import pallas as pl
from jax.experimental.pallas import tpu as pltpu

F32 = jnp.float32
BF16 = jnp.bfloat16

CHUNK = 128
SGU_GROUPS = 8
B_HD2 = 256
B_HD = B_HD2 // 2
MEM_HEADS = 4
N_BUCKETS = 32
MAX_DISTANCE = 128
PAGE_SIZE = 128
EPS = 1e-6
NEG = -1e30

VMEM_BYTES_V7X = 64 * 1024 * 1024
VMEM_LIMIT = VMEM_BYTES_V7X - 8 * 1024 * 1024

NT_DIMS = (((1,), (1,)), ((), ()))


def _params(*sem):
    return pltpu.CompilerParams(dimension_semantics=sem, vmem_limit_bytes=VMEM_LIMIT)


def _normalised_rows(x_ref, g_ref, h_ref):
    @pl.when(pl.program_id(1) == 0)
    def _():
        x = x_ref[...]
        ms = jnp.mean(x * x, axis=-1, keepdims=True)
        h_ref[...] = (x * lax.rsqrt(ms + EPS) * g_ref[...]).astype(BF16)

    return h_ref[...]


def _rms_matmul_kernel(x_ref, g_ref, w_ref, o_ref, h_ref):
    h = _normalised_rows(x_ref, g_ref, h_ref)
    o_ref[...] = jnp.dot(h, w_ref[...], preferred_element_type=F32).astype(o_ref.dtype)


def rms_matmul(x, g, w, *, tm, tn, name):
    M, D = x.shape
    N = w.shape[1]
    return pl.pallas_call(
        _rms_matmul_kernel,
        grid=(M // tm, N // tn),
        in_specs=[pl.BlockSpec((tm, D), lambda i, j: (i, 0)),
                  pl.BlockSpec((1, D), lambda i, j: (0, 0)),
                  pl.BlockSpec((D, tn), lambda i, j: (0, j))],
        out_specs=pl.BlockSpec((tm, tn), lambda i, j: (i, j)),
        out_shape=jax.ShapeDtypeStruct((M, N), F32),
        scratch_shapes=[pltpu.VMEM((tm, D), BF16)],
        compiler_params=_params("parallel", "arbitrary"),
        name=name,
    )(x, g.reshape(1, D), w)


def _rms_matmul_cast_kernel(x_ref, g_ref, w_ref, o_ref, wb_ref, h_ref):
    h = _normalised_rows(x_ref, g_ref, h_ref)
    wb = w_ref[...].astype(BF16)
    wb_ref[...] = wb
    o_ref[...] = jnp.dot(h, wb, preferred_element_type=F32)


def _f32_weight_spec(w, layer, tn, first_block=0):
    if w.ndim == 2:
        return pl.BlockSpec((w.shape[0], tn), lambda i, j: (0, first_block + j))
    return pl.BlockSpec((None, w.shape[1], tn), lambda i, j: (layer, 0, first_block + j))


def rms_matmul_cast(x, g, w, *, tn, name, layer=0, col0=0, n_cols=None):
    M, D = x.shape
    n_cols = w.shape[-1] if n_cols is None else n_cols
    return pl.pallas_call(
        _rms_matmul_cast_kernel,
        grid=(1, n_cols // tn),
        in_specs=[pl.BlockSpec((M, D), lambda i, j: (0, 0)),
                  pl.BlockSpec((1, D), lambda i, j: (0, 0)),
                  _f32_weight_spec(w, layer, tn, col0 // tn)],
        out_specs=[pl.BlockSpec((M, tn), lambda i, j: (0, j)),
                   pl.BlockSpec((D, tn), lambda i, j: (0, j))],
        out_shape=[jax.ShapeDtypeStruct((M, n_cols), F32),
                   jax.ShapeDtypeStruct((D, n_cols), BF16)],
        scratch_shapes=[pltpu.VMEM((M, D), BF16)],
        compiler_params=_params("arbitrary", "arbitrary"),
        name=name,
    )(x, g.reshape(1, D), w)


def rms_matmul_stacked(x, g, w, *, tm, tn, name):
    M, D = x.shape
    G, _, N = w.shape
    nb = N // tn
    return pl.pallas_call(
        _rms_matmul_kernel,
        grid=(M // tm, G * nb),
        in_specs=[pl.BlockSpec((tm, D), lambda i, j: (i, 0)),
                  pl.BlockSpec((1, D), lambda i, j: (0, 0)),
                  pl.BlockSpec((None, D, tn), lambda i, j: (j // nb, 0, j % nb))],
        out_specs=pl.BlockSpec((None, tm, tn), lambda i, j: (j, i, 0)),
        out_shape=jax.ShapeDtypeStruct((G * nb, M, tn), F32),
        scratch_shapes=[pltpu.VMEM((tm, D), BF16)],
        compiler_params=_params("parallel", "arbitrary"),
        name=name,
    )(x, g.reshape(1, D), w)


def _k_proj_kernel(x_ref, g_ref, w_ref, knat_ref, katt_ref, h_ref):
    tm = x_ref.shape[0]
    heads = katt_ref.shape[0]
    j = pl.program_id(1)
    h = _normalised_rows(x_ref, g_ref, h_ref)
    res = jnp.dot(h, w_ref[...], preferred_element_type=F32)
    n_sub = knat_ref.shape[0] // tm
    for q in range(2 * heads):
        knat_ref[pl.ds(j * 2 * heads + q, tm, stride=n_sub), :] = res[:, q * B_HD:(q + 1) * B_HD]
    for hh in range(heads):
        katt_ref[hh] = res[:, hh * B_HD2:(hh + 1) * B_HD2].astype(BF16)


def k_proj_prompt(x, g, w, *, B, tm, heads, name):
    M, D = x.shape
    L = M // B
    N = w.shape[1]
    H = N // B_HD2
    nlb = L // tm
    return pl.pallas_call(
        _k_proj_kernel,
        grid=(M // tm, H // heads),
        in_specs=[pl.BlockSpec((tm, D), lambda i, j: (i, 0)),
                  pl.BlockSpec((1, D), lambda i, j: (0, 0)),
                  pl.BlockSpec((D, heads * B_HD2), lambda i, j: (0, j))],
        out_specs=[pl.BlockSpec((tm * 2 * H, B_HD), lambda i, j: (i, 0)),
                   pl.BlockSpec((None, heads, tm, B_HD2), lambda i, j: (i // nlb, j, i % nlb, 0))],
        out_shape=[jax.ShapeDtypeStruct((M * 2 * H, B_HD), F32),
                   jax.ShapeDtypeStruct((B, H, L, B_HD2), BF16)],
        scratch_shapes=[pltpu.VMEM((tm, D), BF16)],
        compiler_params=_params("parallel", "arbitrary"),
        name=name,
    )(x, g.reshape(1, D), w)


def _v_proj_kernel(x_ref, g_ref, w_ref, v_ref, h_ref):
    h = _normalised_rows(x_ref, g_ref, h_ref)
    res = jnp.dot(h, w_ref[...], preferred_element_type=F32)
    for hh in range(v_ref.shape[0]):
        v_ref[hh] = res[:, hh * B_HD2:(hh + 1) * B_HD2]


def v_proj_prompt(x, g, w, *, B, tm, heads, name):
    M, D = x.shape
    L = M // B
    H = w.shape[1] // B_HD2
    nlb = L // tm
    return pl.pallas_call(
        _v_proj_kernel,
        grid=(M // tm, H // heads),
        in_specs=[pl.BlockSpec((tm, D), lambda i, j: (i, 0)),
                  pl.BlockSpec((1, D), lambda i, j: (0, 0)),
                  pl.BlockSpec((D, heads * B_HD2), lambda i, j: (0, j))],
        out_specs=pl.BlockSpec((None, heads, tm, B_HD2), lambda i, j: (i // nlb, j, i % nlb, 0)),
        out_shape=jax.ShapeDtypeStruct((B, H, L, B_HD2), F32),
        scratch_shapes=[pltpu.VMEM((tm, D), BF16)],
        compiler_params=_params("parallel", "arbitrary"),
        name=name,
    )(x, g.reshape(1, D), w)


def _out_proj_kernel(a_ref, m_ref, w_ref, x_ref, o_ref):
    ka = a_ref.shape[1]
    acc = jnp.dot(a_ref[...], w_ref[:ka, :], preferred_element_type=F32)
    acc = acc + jnp.dot(m_ref[...], w_ref[ka:, :], preferred_element_type=F32)
    o_ref[...] = x_ref[...] + acc


def out_proj(a, m, w, x, *, tm, tn, name):
    M, ka = a.shape
    km = m.shape[1]
    N = w.shape[1]
    return pl.pallas_call(
        _out_proj_kernel,
        grid=(M // tm, N // tn),
        in_specs=[pl.BlockSpec((tm, ka), lambda i, j: (i, 0)),
                  pl.BlockSpec((tm, km), lambda i, j: (i, 0)),
                  pl.BlockSpec((ka + km, tn), lambda i, j: (0, j)),
                  pl.BlockSpec((tm, tn), lambda i, j: (i, j))],
        out_specs=pl.BlockSpec((tm, tn), lambda i, j: (i, j)),
        out_shape=jax.ShapeDtypeStruct((M, N), F32),
        compiler_params=_params("parallel", "arbitrary"),
        name=name,
    )(a, m, w, x)


def _out_proj_cast_kernel(a_ref, m_ref, w_ref, x_ref, o_ref, wb_ref):
    ka = a_ref.shape[1]
    wb = w_ref[...].astype(BF16)
    wb_ref[...] = wb
    acc = jnp.dot(a_ref[...], wb[:ka], preferred_element_type=F32)
    acc = acc + jnp.dot(m_ref[...], wb[ka:], preferred_element_type=F32)
    o_ref[...] = x_ref[...] + acc


def out_proj_cast(a, m, w, x, *, tn, name, layer=0):
    M, ka = a.shape
    km = m.shape[1]
    N = w.shape[-1]
    return pl.pallas_call(
        _out_proj_cast_kernel,
        grid=(1, N // tn),
        in_specs=[pl.BlockSpec((M, ka), lambda i, j: (0, 0)),
                  pl.BlockSpec((M, km), lambda i, j: (0, 0)),
                  _f32_weight_spec(w, layer, tn),
                  pl.BlockSpec((M, tn), lambda i, j: (0, j))],
        out_specs=[pl.BlockSpec((M, tn), lambda i, j: (0, j)),
                   pl.BlockSpec((ka + km, tn), lambda i, j: (0, j))],
        out_shape=[jax.ShapeDtypeStruct((M, N), F32),
                   jax.ShapeDtypeStruct((ka + km, N), BF16)],
        compiler_params=_params("arbitrary", "arbitrary"),
        name=name,
    )(a, m, w, x)


def _out_proj_norm_kernel(a_ref, m_ref, w_ref, x_ref, g_ref, o_ref):
    j = pl.program_id(1)
    ka = a_ref.shape[1]
    tn = x_ref.shape[1]
    acc = jnp.dot(a_ref[...], w_ref[:ka, :], preferred_element_type=F32)
    acc = acc + jnp.dot(m_ref[...], w_ref[ka:, :], preferred_element_type=F32)
    o_ref[:, pl.ds(pl.multiple_of(j * tn, tn), tn)] = x_ref[...] + acc

    @pl.when(j == pl.num_programs(1) - 1)
    def _():
        y = o_ref[...]
        ms = jnp.mean(y * y, axis=-1, keepdims=True)
        o_ref[...] = y * lax.rsqrt(ms + EPS) * g_ref[...]


def out_proj_norm(a, m, w, x, g, *, tm, tn, name):
    M, ka = a.shape
    km = m.shape[1]
    N = w.shape[-1]
    return pl.pallas_call(
        _out_proj_norm_kernel,
        grid=(M // tm, N // tn),
        in_specs=[pl.BlockSpec((tm, ka), lambda i, j: (i, 0)),
                  pl.BlockSpec((tm, km), lambda i, j: (i, 0)),
                  pl.BlockSpec((ka + km, tn), lambda i, j: (0, j)),
                  pl.BlockSpec((tm, tn), lambda i, j: (i, j)),
                  pl.BlockSpec((1, N), lambda i, j: (0, 0))],
        out_specs=pl.BlockSpec((tm, N), lambda i, j: (i, 0)),
        out_shape=jax.ShapeDtypeStruct((M, N), F32),
        compiler_params=_params("parallel", "arbitrary"),
        name=name,
    )(a, m, w, x, g.reshape(1, N))


def _rmsnorm_kernel(x_ref, g_ref, o_ref):
    x = x_ref[...]
    ms = jnp.mean(x * x, axis=-1, keepdims=True)
    o_ref[...] = x * lax.rsqrt(ms + EPS) * g_ref[...]


def rmsnorm_rows(x, g, *, tm, name):
    M, D = x.shape
    tm = min(tm, M)
    return pl.pallas_call(
        _rmsnorm_kernel,
        grid=(M // tm,),
        in_specs=[pl.BlockSpec((tm, D), lambda i: (i, 0)),
                  pl.BlockSpec((1, D), lambda i: (0, 0))],
        out_specs=pl.BlockSpec((tm, D), lambda i: (i, 0)),
        out_shape=jax.ShapeDtypeStruct((M, D), F32),
        compiler_params=_params("parallel"),
        name=name,
    )(x, g.reshape(1, D))


def _layernorm(v, g, b):
    mu = jnp.mean(v, axis=-1, keepdims=True)
    vc = v - mu
    var = jnp.mean(vc * vc, axis=-1, keepdims=True)
    return vc * lax.rsqrt(var + EPS) * g + b


def _sgu_kernel(u_ref, v_ref, z_ref, lng_ref, lnb_ref, w_ref, bt_ref, o_ref):
    rows, width = u_ref.shape
    ch = width // SGU_GROUPS
    r = lax.broadcasted_iota(jnp.int32, (CHUNK, CHUNK), 0)
    c = lax.broadcasted_iota(jnp.int32, (CHUNK, CHUNK), 1)
    causal = c <= r
    ws = [jnp.where(causal, w_ref[g], 0.0).astype(BF16) for g in range(SGU_GROUPS)]
    for n in range(rows // CHUNK):
        rs = slice(n * CHUNK, (n + 1) * CHUNK)
        vn = _layernorm(jax.nn.gelu(v_ref[rs, :]), lng_ref[...], lnb_ref[...]).astype(BF16)
        for g in range(SGU_GROUPS):
            cs = slice(g * ch, (g + 1) * ch)
            mixed = jnp.dot(ws[g], vn[:, cs], preferred_element_type=F32) + bt_ref[:, g:g + 1]
            u = jax.nn.gelu(u_ref[rs, cs])
            o_ref[rs, cs] = (u * mixed * jax.nn.silu(z_ref[rs, cs])).astype(o_ref.dtype)


def sgu_prompt(p, ln_g, ln_b, w_s, b_s, *, width, rows, name):
    M = p.shape[0]
    return pl.pallas_call(
        _sgu_kernel,
        grid=(M // rows,),
        in_specs=[pl.BlockSpec((rows, width), lambda i: (i, 0)),
                  pl.BlockSpec((rows, width), lambda i: (i, 1)),
                  pl.BlockSpec((rows, width), lambda i: (i, 2)),
                  pl.BlockSpec((1, width), lambda i: (0, 0)),
                  pl.BlockSpec((1, width), lambda i: (0, 0)),
                  pl.BlockSpec((SGU_GROUPS, CHUNK, CHUNK), lambda i: (0, 0, 0)),
                  pl.BlockSpec((CHUNK, SGU_GROUPS), lambda i: (0, 0))],
        out_specs=pl.BlockSpec((rows, width), lambda i: (i, 0)),
        out_shape=jax.ShapeDtypeStruct((M, width), BF16),
        compiler_params=_params("parallel"),
        name=name,
    )(p, p, p, ln_g.reshape(1, width), ln_b.reshape(1, width), w_s, b_s.T)


def _sgu_first_row_kernel(u_ref, v_ref, z_ref, lng_ref, lnb_ref, w0_ref, b0_ref, o_ref, vn_ref):
    vn = _layernorm(jax.nn.gelu(v_ref[...]), lng_ref[...], lnb_ref[...])
    vn_ref[...] = vn
    mixed = w0_ref[...] * vn + b0_ref[...]
    o_ref[...] = (jax.nn.gelu(u_ref[...]) * mixed * jax.nn.silu(z_ref[...])).astype(o_ref.dtype)


def sgu_first_row(p, ln_g, ln_b, w_s, b_s, *, width, name):
    M = p.shape[0]
    ch = width // SGU_GROUPS
    w0 = jnp.repeat(w_s[:, 0, 0], ch).reshape(1, width)
    b0 = jnp.repeat(b_s[:, 0], ch).reshape(1, width)
    row = pl.BlockSpec((1, width), lambda i: (0, 0))
    return pl.pallas_call(
        _sgu_first_row_kernel,
        grid=(1,),
        in_specs=[pl.BlockSpec((M, width), lambda i: (0, 0)),
                  pl.BlockSpec((M, width), lambda i: (0, 1)),
                  pl.BlockSpec((M, width), lambda i: (0, 2)),
                  row, row, row, row],
        out_specs=[pl.BlockSpec((M, width), lambda i: (0, 0)),
                   pl.BlockSpec((M, width), lambda i: (0, 0))],
        out_shape=[jax.ShapeDtypeStruct((M, width), BF16),
                   jax.ShapeDtypeStruct((M, width), F32)],
        compiler_params=_params("arbitrary"),
        name=name,
    )(p, p, p, ln_g.reshape(1, width), ln_b.reshape(1, width), w0, b0)


def _mem_attn_kernel(q_ref, zm_ref, k_ref, v_ref, o_ref):
    tm, width = q_ref.shape
    hd = width // MEM_HEADS
    q = q_ref[...]
    if tm < 8:
        q = jnp.broadcast_to(q[0:1], (8, width))
    for h in range(MEM_HEADS):
        cs = slice(h * hd, (h + 1) * hd)
        s = lax.dot_general(q[:, cs].astype(BF16), k_ref[:, cs].astype(BF16), NT_DIMS,
                            preferred_element_type=F32) * (hd ** -0.5)
        e = jnp.exp(s - jnp.max(s, axis=-1, keepdims=True))
        p = e * (1.0 / jnp.sum(e, axis=-1, keepdims=True))
        o = jnp.dot(p.astype(BF16), v_ref[:, cs].astype(BF16), preferred_element_type=F32)
        o_ref[:, cs] = (o[:tm] * jax.nn.silu(zm_ref[:, cs])).astype(o_ref.dtype)


def mem_attention(p, mem_k, mem_v, *, lead_k, lead_v, q_block, tm, name):
    B, L, _ = p.shape
    mlen, width = mem_k.shape[-2:]
    tm = min(tm, L)
    mem_spec = lambda lead: pl.BlockSpec((None,) * (len(lead) + 1) + (mlen, width),
                                         lambda b, i: (*lead, b, 0, 0))
    return pl.pallas_call(
        _mem_attn_kernel,
        grid=(B, L // tm),
        in_specs=[pl.BlockSpec((None, tm, width), lambda b, i: (b, i, q_block)),
                  pl.BlockSpec((None, tm, width), lambda b, i: (b, i, q_block + 1)),
                  mem_spec(lead_k), mem_spec(lead_v)],
        out_specs=pl.BlockSpec((None, tm, width), lambda b, i: (b, i, 0)),
        out_shape=jax.ShapeDtypeStruct((B, L, width), BF16),
        compiler_params=_params("parallel", "parallel"),
        name=name,
    )(p, p, mem_k, mem_v)


def _rel_bucket(dist):
    n = jnp.maximum(dist, 0)
    max_exact = N_BUCKETS // 2
    nf = jnp.maximum(n, 1).astype(F32)
    large = max_exact + (jnp.log(nf / max_exact) / math.log(MAX_DISTANCE / max_exact)
                         * (N_BUCKETS - max_exact)).astype(jnp.int32)
    large = jnp.minimum(large, N_BUCKETS - 1)
    return jnp.where(n < max_exact, n, large)


def _lambda(lam_ref, lam_init):
    a = jnp.sum(lam_ref[0:1, :] * lam_ref[1:2, :], axis=-1, keepdims=True)
    b = jnp.sum(lam_ref[2:3, :] * lam_ref[3:4, :], axis=-1, keepdims=True)
    return jnp.exp(a) - jnp.exp(b) + lam_init


def _subln_gate(o, g, z, lam_init):
    ms = jnp.mean(o * o, axis=-1, keepdims=True)
    on = (o * lax.rsqrt(ms + EPS) * g) * (1.0 - lam_init)
    return on * jax.nn.silu(z)


def _lane_groups(x):
    return [x[:, g * 128:(g + 1) * 128] for g in range(x.shape[1] // 128)]


def _diff_attn_kernel(q_ref, z_ref, k_ref, v_ref, bd_ref, be_ref, bfar_ref, lam_ref, g_ref, o_ref,
                      vb_ref, b0_ref, b1_ref, s_ref, mp_ref, lp_ref, acc_ref, *, lam_init):
    t = q_ref.shape[0]
    qi = pl.program_id(2)
    scale = B_HD ** -0.5
    md = MAX_DISTANCE
    nb = t // md

    @pl.when(qi == 0)
    def _():
        vb_ref[...] = v_ref[...].astype(BF16)
        far = jnp.broadcast_to(bfar_ref[0:1, 0:1], (md, md))
        masked = jnp.full((md, md), NEG, F32)
        for bi in range(nb):
            for bj in range(nb):
                piece = bd_ref[...] if bi == bj else be_ref[...] if bi == bj + 1 else far if bi > bj else masked
                b0_ref[bi * md:(bi + 1) * md, bj * md:(bj + 1) * md] = piece
                piece = be_ref[...] if (bi == 0 and bj == nb - 1) else far
                b1_ref[bi * md:(bi + 1) * md, bj * md:(bj + 1) * md] = piece

    q = q_ref[...]
    qs = (q[:, :B_HD].astype(BF16), q[:, B_HD:].astype(BF16))
    mp_ref[...] = jnp.full(mp_ref.shape, NEG, F32)
    lp_ref[...] = jnp.zeros(lp_ref.shape, F32)
    acc_ref[...] = jnp.zeros(acc_ref.shape, F32)

    def scores(kj, bias):
        off = pl.multiple_of(kj * t, t)
        k = k_ref[pl.ds(off, t), :]
        for c in range(2):
            s = lax.dot_general(qs[c], k[:, c * B_HD:(c + 1) * B_HD], NT_DIMS,
                                preferred_element_type=F32) * scale + bias
            s_ref[c, :, pl.ds(off, t)] = s
            mp_ref[c] = functools.reduce(jnp.maximum, _lane_groups(s), mp_ref[c])

    bfar = bfar_ref[0:1, 0:1]

    def far(kj, carry):
        scores(kj, bfar)
        return carry

    lax.fori_loop(0, jnp.maximum(qi - 1, 0), far, 0)

    @pl.when(qi >= 1)
    def _():
        scores(qi - 1, b1_ref[...])

    scores(qi, b0_ref[...])

    m_rows = [jnp.broadcast_to(jnp.max(mp_ref[c], axis=-1, keepdims=True), (t, 128)) for c in range(2)]

    def accumulate(kj, carry):
        off = pl.multiple_of(kj * t, t)
        v = vb_ref[pl.ds(off, t), :]
        for c in range(2):
            ps = [jnp.exp(sg - m_rows[c]) for sg in _lane_groups(s_ref[c, :, pl.ds(off, t)])]
            lp_ref[c] = functools.reduce(jnp.add, ps, lp_ref[c])
            p = jnp.concatenate(ps, axis=1).astype(BF16)
            acc_ref[c] = acc_ref[c] + jnp.dot(p, v, preferred_element_type=F32)
        return carry

    lax.fori_loop(0, qi + 1, accumulate, 0)

    lam = _lambda(lam_ref, lam_init)
    inv_l = [1.0 / jnp.sum(lp_ref[c], axis=-1, keepdims=True) for c in range(2)]
    o = acc_ref[0] * inv_l[0] - lam * (acc_ref[1] * inv_l[1])
    o_ref[...] = _subln_gate(o, g_ref[...], z_ref[...], lam_init).astype(o_ref.dtype)


def _toeplitz_tiles(g, t):
    H, P = g.shape
    hankel = jnp.tile(g[:, ::-1], (1, t + 1))[:, :t * (P + 1)].reshape(H, t, P + 1)
    return hankel[:, ::-1, :t]


def diff_attention_prompt(p, k, v, rel_bias, lam_rows, subln_g, *, lam_init, t, name):
    B, L, _ = p.shape
    H = k.shape[1]
    md = MAX_DISTANCE
    assert t % md == 0 and L % t == 0
    table = rel_bias.astype(F32)
    d = jnp.arange(-(md - 1), md, dtype=jnp.int32)
    bd = _toeplitz_tiles(jnp.where((d >= 0)[None], table[_rel_bucket(d)].T, NEG), md)
    be = _toeplitz_tiles(table[_rel_bucket(d + md)].T, md)
    bfar = jnp.broadcast_to(table[_rel_bucket(jnp.int32(md + 1))][:, None, None], (H, 1, 128))
    blk = lambda off: pl.BlockSpec((None, t, B_HD2), lambda b, h, i: (b, i, off + h))
    kv = pl.BlockSpec((None, None, L, B_HD2), lambda b, h, i: (b, h, 0, 0))
    tile = pl.BlockSpec((None, md, md), lambda b, h, i: (h, 0, 0))
    return pl.pallas_call(
        functools.partial(_diff_attn_kernel, lam_init=lam_init),
        grid=(B, H, L // t),
        in_specs=[blk(0), blk(H), kv, kv, tile, tile,
                  pl.BlockSpec((None, 1, 128), lambda b, h, i: (h, 0, 0)),
                  pl.BlockSpec((4, B_HD), lambda b, h, i: (0, 0)),
                  pl.BlockSpec((1, B_HD2), lambda b, h, i: (0, 0))],
        out_specs=pl.BlockSpec((None, t, B_HD2), lambda b, h, i: (b, i, h)),
        out_shape=jax.ShapeDtypeStruct((B, L, H * B_HD2), BF16),
        scratch_shapes=[pltpu.VMEM((L, B_HD2), BF16), pltpu.VMEM((t, t), F32), pltpu.VMEM((t, t), F32),
                        pltpu.VMEM((2, t, L), F32), pltpu.VMEM((2, t, 128), F32), pltpu.VMEM((2, t, 128), F32),
                        pltpu.VMEM((2, t, B_HD2), F32)],
        compiler_params=_params("parallel", "parallel", "arbitrary"),
        name=name,
    )(p, p, k, v, bd, be, bfar, lam_rows, subln_g.reshape(1, B_HD2))


def _in_proj_decode_kernel(pt_ref, x_ref, g_ref, w_ref, ps_ref, kn_ref, vn_ref, sb_ref, lam_ref, sg_ref, *rest,
                           pps, lam_init, steps_per_seq):
    del pt_ref
    k_pages, v_pages = rest[:pps], rest[pps:2 * pps]
    o_ref, d_ref, h_ref, qb_ref, m_ref, l_ref, acc_ref = rest[2 * pps:]
    H, rows, _ = qb_ref.shape
    width = H * B_HD2
    step = pl.program_id(0) * pl.num_programs(1) + pl.program_id(1)
    b = step // steps_per_seq
    first = step % steps_per_seq == 0
    last = step % steps_per_seq == steps_per_seq - 1
    scale = B_HD ** -0.5

    def query_rows(hd):
        r_id = lax.broadcasted_iota(jnp.int32, (rows, B_HD2), 0)
        c_id = lax.broadcasted_iota(jnp.int32, (rows, B_HD2), 1)
        own = (c_id >> 7) == r_id
        return jnp.where(own, ps_ref[pl.ds(b, 1), hd * B_HD2:(hd + 1) * B_HD2], 0.0)

    hx = _normalised_rows(x_ref, g_ref, h_ref)

    @pl.when(first)
    def _():
        for hd in range(H):
            qb_ref[hd] = query_rows(hd).astype(BF16)
        m_ref[...] = jnp.full(m_ref.shape, NEG, F32)
        l_ref[...] = jnp.zeros(l_ref.shape, F32)
        acc_ref[...] = jnp.zeros(acc_ref.shape, F32)

    def softmax_step(hd, s):
        m_old = m_ref[hd]
        m_new = jnp.maximum(m_old, jnp.max(s, axis=-1, keepdims=True))
        alpha = jnp.exp(m_old - m_new)
        p = jnp.exp(s - m_new)
        l_ref[hd] = alpha * l_ref[hd] + jnp.sum(p, axis=-1, keepdims=True)
        m_ref[hd] = m_new
        return alpha, p

    khs = [jnp.concatenate(
        [jnp.concatenate([kp[pl.ds(2 * hd + c, PAGE_SIZE, stride=2 * H), :] for c in range(2)], axis=1)
         for kp in k_pages], axis=0).astype(BF16) for hd in range(H)]
    vhs = [jnp.concatenate([vp[hd] for vp in v_pages], axis=0).astype(BF16) for hd in range(H)]
    half = x_ref.shape[0] // 2
    o_ref[:half, :] = jnp.dot(hx[:half], w_ref[...], preferred_element_type=F32)
    scores = []
    for hd in range(H):
        bias = jnp.concatenate([sb_ref[0, hd:hd + 1, :]] * (pps - 1)
                               + [jnp.where(last, sb_ref[1, hd:hd + 1, :], sb_ref[0, hd:hd + 1, :])], axis=1)
        scores.append(lax.dot_general(qb_ref[hd], khs[hd], NT_DIMS, preferred_element_type=F32) * scale + bias)
    o_ref[half:, :] = jnp.dot(hx[half:], w_ref[...], preferred_element_type=F32)
    folded = [softmax_step(hd, scores[hd]) for hd in range(H)]
    for hd in range(H):
        alpha, p = folded[hd]
        acc_ref[hd] = alpha * acc_ref[hd] + jnp.dot(p.astype(BF16), vhs[hd], preferred_element_type=F32)

    @pl.when(last)
    def _():
        lam = _lambda(lam_ref, lam_init)
        for hd in range(H):
            cs = slice(hd * B_HD2, (hd + 1) * B_HD2)
            s = (jnp.sum(query_rows(hd) * kn_ref[pl.ds(b, 1), cs], axis=-1, keepdims=True) * scale
                 + sb_ref[2, hd:hd + 1, 0:1])
            alpha, p = softmax_step(hd, s)
            o_both = (alpha * acc_ref[hd] + p * vn_ref[pl.ds(b, 1), cs]) * (1.0 / l_ref[hd])
            o = o_both[0:1] - lam * o_both[1:2]
            z = ps_ref[pl.ds(b, 1), width + hd * B_HD2:width + (hd + 1) * B_HD2]
            d_ref[pl.ds(b, 1), cs] = _subln_gate(o, sg_ref[...], z, lam_init)


def in_proj_with_decode(x, g, w, p_s, k_new, v_new, cache_k, cache_v, page_table, rel_bias, lam_rows, subln_g,
                        *, lam_init, tm, tn, name):
    M, D = x.shape
    N = w.shape[1]
    B, n_pages = page_table.shape
    width = k_new.shape[-1]
    H = width // B_HD2
    k_rows = PAGE_SIZE * 2 * H
    ni, nj = M // tm, N // tn
    assert (B * n_pages) % (ni * nj) == 0, "every grid step takes the same number of pages"
    pps = B * n_pages // (ni * nj)
    assert n_pages % pps == 0 and PAGE_SIZE >= MAX_DISTANCE
    rows = 8
    table = rel_bias.astype(F32)
    dist_last = PAGE_SIZE - jnp.arange(PAGE_SIZE, dtype=jnp.int32)
    sb = jnp.stack([
        jnp.broadcast_to(table[_rel_bucket(jnp.int32(PAGE_SIZE + 1))][:, None], (H, PAGE_SIZE)),
        table[_rel_bucket(dist_last)].T,
        jnp.broadcast_to(table[_rel_bucket(jnp.int32(0))][:, None], (H, PAGE_SIZE)),
    ])
    sg = subln_g.reshape(1, B_HD2)
    whole = lambda a: pl.BlockSpec(a.shape, lambda i, j, pt: (0,) * a.ndim)
    k_page = lambda r: pl.BlockSpec((k_rows, B_HD), lambda i, j, pt: (pt[(i * nj + j) * pps + r], 0))
    v_page = lambda r: pl.BlockSpec((None, H, PAGE_SIZE, B_HD2),
                                    lambda i, j, pt: (pt[(i * nj + j) * pps + r], 0, 0, 0))
    grid_spec = pltpu.PrefetchScalarGridSpec(
        num_scalar_prefetch=1,
        grid=(ni, nj),
        in_specs=[pl.BlockSpec((tm, D), lambda i, j, pt: (i, 0)),
                  pl.BlockSpec((1, D), lambda i, j, pt: (0, 0)),
                  pl.BlockSpec((D, tn), lambda i, j, pt: (0, j)),
                  whole(p_s), whole(k_new), whole(v_new), whole(sb), whole(lam_rows), whole(sg)]
                 + [k_page(r) for r in range(pps)] + [v_page(r) for r in range(pps)],
        out_specs=[pl.BlockSpec((tm, tn), lambda i, j, pt: (i, j)),
                   pl.BlockSpec((B, width), lambda i, j, pt: (0, 0))],
        scratch_shapes=[pltpu.VMEM((tm, D), BF16), pltpu.VMEM((H, rows, B_HD2), BF16),
                        pltpu.VMEM((H, rows, 1), F32), pltpu.VMEM((H, rows, 1), F32),
                        pltpu.VMEM((H, rows, B_HD2), F32)],
    )
    return pl.pallas_call(
        functools.partial(_in_proj_decode_kernel, pps=pps, lam_init=lam_init, steps_per_seq=n_pages // pps),
        grid_spec=grid_spec,
        out_shape=[jax.ShapeDtypeStruct((M, N), F32), jax.ShapeDtypeStruct((B, width), F32)],
        compiler_params=_params("arbitrary", "arbitrary"),
        name=name,
    )(page_table.reshape(-1), x, g.reshape(1, D), w, p_s, k_new, v_new, sb, lam_rows, sg,
      *([cache_k] * pps), *([cache_v] * pps))


def _trunk(x_prompt, x_sample, mkv, smk, smv, past, W):
    bp, L, D = x_prompt.shape
    bs = x_sample.shape[0]
    mp = bp * L
    depth = W["norm_g"].shape[0]
    n_a = W["w_in_a"].shape[0]
    e_a = W["sgu_ln_g"].shape[1]
    b_width = W["w_kv"].shape[1] // 2
    mem_width = mkv.shape[-1]
    H = b_width // B_HD2
    tm, tn = 512, 1024
    tn_cast = 512
    xp = x_prompt.reshape(mp, D)
    xs = x_sample.reshape(bs, D)
    chunk_vs = []

    def mem_both(l, pp, ps, q_block):
        mo_p = mem_attention(pp.reshape(bp, L, -1), mkv, mkv, lead_k=(l, 0), lead_v=(l, 1),
                             q_block=q_block, tm=512, name=f"mem{l}_p")
        mo_s = mem_attention(ps.reshape(bs, 1, -1), smk, smv, lead_k=(l,), lead_v=(l,),
                             q_block=q_block, tm=1, name=f"mem{l}_s")
        return mo_p.reshape(mp, mem_width), mo_s.reshape(bs, mem_width)

    for l in range(depth):
        g_in = W["norm_g"][l]
        if l < n_a:
            ps, wb = rms_matmul_cast(xs, g_in, W["w_in_a"], layer=l, tn=tn_cast, name=f"in_a{l}_s")
            pp = rms_matmul(xp, g_in, wb, tm=tm, tn=tn, name=f"in_a{l}_p")
            sgu_w = (W["sgu_ln_g"][l], W["sgu_ln_b"][l], W["sgu_w"][l], W["sgu_b"][l])
            main_s, vn = sgu_first_row(ps, *sgu_w, width=e_a, name=f"sgu{l}_s")
            chunk_vs.append(vn)
            main_p = sgu_prompt(pp, *sgu_w, width=e_a, rows=2 * CHUNK, name=f"sgu{l}_p")
            q_block = 3 * e_a // mem_width
            w_out, w_layer = W["w_out_a"], l
        else:
            jb = l - n_a
            lam_init = 0.8 - 0.6 * math.exp(-0.3 * l)
            lam_rows = jnp.stack([W["lambda_q1"][jb], W["lambda_k1"][jb], W["lambda_q2"][jb], W["lambda_k2"][jb]])
            ps, wb = rms_matmul_cast(xs, g_in, W["w_in_b"], layer=jb, tn=tn_cast, name=f"in_b{jb}_s")
            pp, main_s = in_proj_with_decode(xp, g_in, wb, ps, k_s, v_s, *past, W["rel_bias"], lam_rows,
                                             W["subln_g"][jb], lam_init=lam_init, tm=tm, tn=tn_cast,
                                             name=f"in_b{jb}_p_dattn{jb}_s")
            main_s = main_s.astype(BF16)
            main_p = diff_attention_prompt(pp.reshape(bp, L, -1), k_att, v_p, W["rel_bias"], lam_rows,
                                           W["subln_g"][jb], lam_init=lam_init, t=512, name=f"dattn{jb}_p")
            main_p = main_p.reshape(mp, b_width)
            q_block = 2 * b_width // mem_width
            w_out, w_layer = W["w_out_b"], jb
        mo_p, mo_s = mem_both(l, pp, ps, q_block)
        xs, wb = out_proj_cast(main_s, mo_s, w_out, xs, layer=w_layer, tn=tn_cast, name=f"out{l}_s")
        if l < depth - 1:
            xp = out_proj(main_p, mo_p, wb, xp, tm=tm, tn=tn, name=f"out{l}_p")
        else:
            y_p = out_proj_norm(main_p, mo_p, wb, xp, W["final_norm_g"], tm=tm, tn=tn, name=f"out{l}_p")
        if l == n_a - 1:
            g_kv = W["kv_norm_g"]
            k_s, wb_k = rms_matmul_cast(xs, g_kv, W["w_kv"], col0=0, n_cols=b_width, tn=tn_cast, name="k_s")
            v_s, wb_v = rms_matmul_cast(xs, g_kv, W["w_kv"], col0=b_width, n_cols=b_width, tn=tn_cast, name="v_s")
            k_p, k_att = k_proj_prompt(xp, g_kv, wb_k, B=bp, tm=tm, heads=2, name="k_p")
            v_p = v_proj_prompt(xp, g_kv, wb_v, B=bp, tm=tm, heads=4, name="v_p")
    y_s = rmsnorm_rows(xs, W["final_norm_g"], tm=bs, name="final_s")
    return (y_p.reshape(bp, L, D), y_s.reshape(bs, 1, D),
            k_p.reshape(bp, L, H, 2, B_HD), jnp.transpose(v_p, (0, 2, 1, 3)),
            k_s.reshape(bs, 1, H, 2, B_HD), v_s.reshape(bs, 1, H, B_HD2), chunk_vs)


def kernel(x_prompt, x_sample, cache_k, cache_v, cache_mem_k, cache_mem_v, page_table, mem_prompt,
           norm_g, final_norm_g, mem_norm_g, w_mem_kv, w_in_a, sgu_ln_g, sgu_ln_b, sgu_w, sgu_b,
           w_out_a, kv_norm_g, w_kv, w_in_b, lambda_q1, lambda_k1, lambda_q2, lambda_k2,
           subln_g, rel_bias, w_out_b):
    depth = norm_g.shape[0]
    b_width = w_kv.shape[1] // 2
    mem_width = w_mem_kv.shape[-1] // 2
    mem_hd = mem_width // MEM_HEADS
    W = dict(norm_g=norm_g, final_norm_g=final_norm_g, sgu_ln_g=sgu_ln_g, sgu_ln_b=sgu_ln_b, sgu_w=sgu_w,
             sgu_b=sgu_b, kv_norm_g=kv_norm_g, lambda_q1=lambda_q1, lambda_k1=lambda_k1, lambda_q2=lambda_q2,
             lambda_k2=lambda_k2, subln_g=subln_g, rel_bias=rel_bias,
             w_in_a=w_in_a, w_out_a=w_out_a, w_in_b=w_in_b, w_out_b=w_out_b, w_kv=w_kv)

    bd, dec_seq = x_sample.shape[:2]
    assert dec_seq == 1, "the sample group is one new token per sequence"
    past = (cache_k.reshape(-1, B_HD), jnp.transpose(cache_v, (0, 2, 1, 3)), page_table)
    smk = cache_mem_k.reshape(depth, bd, -1, mem_width)
    smv = cache_mem_v.reshape(depth, bd, -1, mem_width)

    bp, m_len, d_model = mem_prompt.shape
    mkv = rms_matmul_stacked(mem_prompt.reshape(bp * m_len, d_model), mem_norm_g, w_mem_kv.astype(BF16),
                             tm=512, tn=mem_width, name="mem_kv")
    mkv = mkv.reshape(depth, 2, bp, m_len, mem_width)

    y_p, y_s, k_p, v_p, k_s, v_s, chunk_vs = _trunk(x_prompt, x_sample, mkv, smk, smv, past, W)
    chunk_v_s = jnp.stack(chunk_vs).reshape(len(chunk_vs), bd, dec_seq, -1)
    mem_shape = (depth, bp, m_len, MEM_HEADS, mem_hd)
    return (y_p, y_s, k_p, v_p, k_s, v_s, mkv[:, 0].reshape(mem_shape), mkv[:, 1].reshape(mem_shape), chunk_v_s)
```

```python
import functools
import math

import jax
import jax.numpy as jnp
from jax import lax
from jax.experimental import pallas as pl
from jax.experimental.pallas import tpu as pltpu

F32 = jnp.float32
BF16 = jnp.bfloat16

CHUNK = 128
SGU_GROUPS = 8
B_HD2 = 256
B_HD = B_HD2 // 2
MEM_HEADS = 4
N_BUCKETS = 32
MAX_DISTANCE = 128
PAGE_SIZE = 128
EPS = 1e-6
NEG = -1e30

VMEM_BYTES_V7X = 64 * 1024 * 1024
MXU_DEPTH_V7X = 256
VMEM_LIMIT = VMEM_BYTES_V7X - 8 * 1024 * 1024

NT_DIMS = (((1,), (1,)), ((), ()))


def _params(*sem):
    return pltpu.CompilerParams(dimension_semantics=sem, vmem_limit_bytes=VMEM_LIMIT)


def _normalised_rows(x_ref, g_ref, h_ref):
    @pl.when(pl.program_id(1) == 0)
    def _():
        x = x_ref[...]
        ms = jnp.mean(x * x, axis=-1, keepdims=True)
        h_ref[...] = (x * lax.rsqrt(ms + EPS) * g_ref[...]).astype(BF16)

    return h_ref[...]


def _rms_dot(x_ref, g_ref, w_ref, h_ref, emit):
    first = pl.program_id(1) == 0

    @pl.when(first)
    def _():
        x = x_ref[...]
        r = lax.rsqrt(jnp.mean(x * x, axis=-1, keepdims=True) + EPS)
        kc = min(MXU_DEPTH_V7X, x_ref.shape[1])
        acc = None
        for c in range(x_ref.shape[1] // kc):
            cs = slice(c * kc, (c + 1) * kc)
            hc = (x_ref[:, cs] * r * g_ref[:, cs]).astype(BF16)
            h_ref[:, cs] = hc
            part = jnp.dot(hc, w_ref[cs, :], preferred_element_type=F32)
            acc = part if acc is None else acc + part
        emit(acc)

    @pl.when(jnp.logical_not(first))
    def _():
        emit(jnp.dot(h_ref[...], w_ref[...], preferred_element_type=F32))


def _rms_matmul_kernel(x_ref, g_ref, w_ref, o_ref, h_ref):
    def emit(res):
        o_ref[...] = res

    _rms_dot(x_ref, g_ref, w_ref, h_ref, emit)


def rms_matmul(x, g, w, *, tm, tn, name):
    M, D = x.shape
    N = w.shape[1]
    return pl.pallas_call(
        _rms_matmul_kernel,
        grid=(M // tm, N // tn),
        in_specs=[pl.BlockSpec((tm, D), lambda i, j: (i, 0)),
                  pl.BlockSpec((1, D), lambda i, j: (0, 0)),
                  pl.BlockSpec((D, tn), lambda i, j: (0, j))],
        out_specs=pl.BlockSpec((tm, tn), lambda i, j: (i, j)),
        out_shape=jax.ShapeDtypeStruct((M, N), F32),
        scratch_shapes=[pltpu.VMEM((tm, D), BF16)],
        compiler_params=_params("parallel", "arbitrary"),
        name=name,
    )(x, g.reshape(1, D), w)


def _rms_matmul_cast_kernel(x_ref, g_ref, w_ref, o_ref, wb_ref, h_ref):
    h = _normalised_rows(x_ref, g_ref, h_ref)
    wb = w_ref[...].astype(BF16)
    wb_ref[...] = wb
    o_ref[...] = jnp.dot(h, wb, preferred_element_type=F32)


def _f32_weight_spec(w, layer, tn, first_block=0):
    if w.ndim == 2:
        return pl.BlockSpec((w.shape[0], tn), lambda i, j: (0, first_block + j))
    return pl.BlockSpec((None, w.shape[1], tn), lambda i, j: (layer, 0, first_block + j))


def rms_matmul_cast(x, g, w, *, tn, name, layer=0, col0=0, n_cols=None):
    M, D = x.shape
    n_cols = w.shape[-1] if n_cols is None else n_cols
    return pl.pallas_call(
        _rms_matmul_cast_kernel,
        grid=(1, n_cols // tn),
        in_specs=[pl.BlockSpec((M, D), lambda i, j: (0, 0)),
                  pl.BlockSpec((1, D), lambda i, j: (0, 0)),
                  _f32_weight_spec(w, layer, tn, col0 // tn)],
        out_specs=[pl.BlockSpec((M, tn), lambda i, j: (0, j)),
                   pl.BlockSpec((D, tn), lambda i, j: (0, j))],
        out_shape=[jax.ShapeDtypeStruct((M, n_cols), F32),
                   jax.ShapeDtypeStruct((D, n_cols), BF16)],
        scratch_shapes=[pltpu.VMEM((M, D), BF16)],
        compiler_params=_params("arbitrary", "arbitrary"),
        name=name,
    )(x, g.reshape(1, D), w)


def rms_matmul_stacked(x, g, w, *, tm, tn, name):
    M, D = x.shape
    G, _, N = w.shape
    nb = N // tn
    return pl.pallas_call(
        _rms_matmul_kernel,
        grid=(M // tm, G * nb),
        in_specs=[pl.BlockSpec((tm, D), lambda i, j: (i, 0)),
                  pl.BlockSpec((1, D), lambda i, j: (0, 0)),
                  pl.BlockSpec((None, D, tn), lambda i, j: (j // nb, 0, j % nb))],
        out_specs=pl.BlockSpec((None, tm, tn), lambda i, j: (j, i, 0)),
        out_shape=jax.ShapeDtypeStruct((G * nb, M, tn), F32),
        scratch_shapes=[pltpu.VMEM((tm, D), BF16)],
        compiler_params=_params("parallel", "arbitrary"),
        name=name,
    )(x, g.reshape(1, D), w)


def _k_proj_kernel(x_ref, g_ref, w_ref, knat_ref, katt_ref, h_ref):
    tm = x_ref.shape[0]
    heads = katt_ref.shape[0]
    j = pl.program_id(1)
    n_sub = knat_ref.shape[0] // tm

    def emit(res):
        for q in range(2 * heads):
            knat_ref[pl.ds(j * 2 * heads + q, tm, stride=n_sub), :] = res[:, q * B_HD:(q + 1) * B_HD]
        for hh in range(heads):
            katt_ref[hh] = res[:, hh * B_HD2:(hh + 1) * B_HD2].astype(BF16)

    _rms_dot(x_ref, g_ref, w_ref, h_ref, emit)


def k_proj_prompt(x, g, w, *, B, tm, heads, name):
    M, D = x.shape
    L = M // B
    N = w.shape[1]
    H = N // B_HD2
    nlb = L // tm
    return pl.pallas_call(
        _k_proj_kernel,
        grid=(M // tm, H // heads),
        in_specs=[pl.BlockSpec((tm, D), lambda i, j: (i, 0)),
                  pl.BlockSpec((1, D), lambda i, j: (0, 0)),
                  pl.BlockSpec((D, heads * B_HD2), lambda i, j: (0, j))],
        out_specs=[pl.BlockSpec((tm * 2 * H, B_HD), lambda i, j: (i, 0)),
                   pl.BlockSpec((None, heads, tm, B_HD2), lambda i, j: (i // nlb, j, i % nlb, 0))],
        out_shape=[jax.ShapeDtypeStruct((M * 2 * H, B_HD), F32),
                   jax.ShapeDtypeStruct((B, H, L, B_HD2), BF16)],
        scratch_shapes=[pltpu.VMEM((tm, D), BF16)],
        compiler_params=_params("parallel", "arbitrary"),
        name=name,
    )(x, g.reshape(1, D), w)


def _v_proj_kernel(x_ref, g_ref, w_ref, v_ref, h_ref):
    def emit(res):
        for hh in range(v_ref.shape[0]):
            v_ref[hh] = res[:, hh * B_HD2:(hh + 1) * B_HD2]

    _rms_dot(x_ref, g_ref, w_ref, h_ref, emit)


def v_proj_prompt(x, g, w, *, B, tm, heads, name):
    M, D = x.shape
    L = M // B
    H = w.shape[1] // B_HD2
    nlb = L // tm
    return pl.pallas_call(
        _v_proj_kernel,
        grid=(M // tm, H // heads),
        in_specs=[pl.BlockSpec((tm, D), lambda i, j: (i, 0)),
                  pl.BlockSpec((1, D), lambda i, j: (0, 0)),
                  pl.BlockSpec((D, heads * B_HD2), lambda i, j: (0, j))],
        out_specs=pl.BlockSpec((None, heads, tm, B_HD2), lambda i, j: (i // nlb, j, i % nlb, 0)),
        out_shape=jax.ShapeDtypeStruct((B, H, L, B_HD2), F32),
        scratch_shapes=[pltpu.VMEM((tm, D), BF16)],
        compiler_params=_params("parallel", "arbitrary"),
        name=name,
    )(x, g.reshape(1, D), w)


def _out_proj_kernel(a_ref, m_ref, w_ref, x_ref, o_ref):
    ka = a_ref.shape[1]
    acc = jnp.dot(a_ref[...], w_ref[:ka, :], preferred_element_type=F32)
    acc = acc + jnp.dot(m_ref[...], w_ref[ka:, :], preferred_element_type=F32)
    o_ref[...] = x_ref[...] + acc


def out_proj(a, m, w, x, *, tm, tn, name):
    M, ka = a.shape
    km = m.shape[1]
    N = w.shape[1]
    return pl.pallas_call(
        _out_proj_kernel,
        grid=(M // tm, N // tn),
        in_specs=[pl.BlockSpec((tm, ka), lambda i, j: (i, 0)),
                  pl.BlockSpec((tm, km), lambda i, j: (i, 0)),
                  pl.BlockSpec((ka + km, tn), lambda i, j: (0, j)),
                  pl.BlockSpec((tm, tn), lambda i, j: (i, j))],
        out_specs=pl.BlockSpec((tm, tn), lambda i, j: (i, j)),
        out_shape=jax.ShapeDtypeStruct((M, N), F32),
        compiler_params=_params("parallel", "arbitrary"),
        name=name,
    )(a, m, w, x)


def _out_proj_cast_kernel(a_ref, m_ref, w_ref, x_ref, o_ref, wb_ref):
    ka = a_ref.shape[1]
    wb = w_ref[...].astype(BF16)
    wb_ref[...] = wb
    acc = jnp.dot(a_ref[...], wb[:ka], preferred_element_type=F32)
    acc = acc + jnp.dot(m_ref[...], wb[ka:], preferred_element_type=F32)
    o_ref[...] = x_ref[...] + acc


def out_proj_cast(a, m, w, x, *, tn, name, layer=0):
    M, ka = a.shape
    km = m.shape[1]
    N = w.shape[-1]
    return pl.pallas_call(
        _out_proj_cast_kernel,
        grid=(1, N // tn),
        in_specs=[pl.BlockSpec((M, ka), lambda i, j: (0, 0)),
                  pl.BlockSpec((M, km), lambda i, j: (0, 0)),
                  _f32_weight_spec(w, layer, tn),
                  pl.BlockSpec((M, tn), lambda i, j: (0, j))],
        out_specs=[pl.BlockSpec((M, tn), lambda i, j: (0, j)),
                   pl.BlockSpec((ka + km, tn), lambda i, j: (0, j))],
        out_shape=[jax.ShapeDtypeStruct((M, N), F32),
                   jax.ShapeDtypeStruct((ka + km, N), BF16)],
        compiler_params=_params("arbitrary", "arbitrary"),
        name=name,
    )(a, m, w, x)


def _out_proj_norm_kernel(a_ref, m_ref, w_ref, x_ref, g_ref, o_ref):
    j = pl.program_id(1)
    ka = a_ref.shape[1]
    tn = x_ref.shape[1]
    acc = jnp.dot(a_ref[...], w_ref[:ka, :], preferred_element_type=F32)
    acc = acc + jnp.dot(m_ref[...], w_ref[ka:, :], preferred_element_type=F32)
    o_ref[:, pl.ds(pl.multiple_of(j * tn, tn), tn)] = x_ref[...] + acc

    @pl.when(j == pl.num_programs(1) - 1)
    def _():
        y = o_ref[...]
        ms = jnp.mean(y * y, axis=-1, keepdims=True)
        o_ref[...] = y * lax.rsqrt(ms + EPS) * g_ref[...]


def out_proj_norm(a, m, w, x, g, *, tm, tn, name):
    M, ka = a.shape
    km = m.shape[1]
    N = w.shape[-1]
    return pl.pallas_call(
        _out_proj_norm_kernel,
        grid=(M // tm, N // tn),
        in_specs=[pl.BlockSpec((tm, ka), lambda i, j: (i, 0)),
                  pl.BlockSpec((tm, km), lambda i, j: (i, 0)),
                  pl.BlockSpec((ka + km, tn), lambda i, j: (0, j)),
                  pl.BlockSpec((tm, tn), lambda i, j: (i, j)),
                  pl.BlockSpec((1, N), lambda i, j: (0, 0))],
        out_specs=pl.BlockSpec((tm, N), lambda i, j: (i, 0)),
        out_shape=jax.ShapeDtypeStruct((M, N), F32),
        compiler_params=_params("parallel", "arbitrary"),
        name=name,
    )(a, m, w, x, g.reshape(1, N))


def _rmsnorm_kernel(x_ref, g_ref, o_ref):
    x = x_ref[...]
    ms = jnp.mean(x * x, axis=-1, keepdims=True)
    o_ref[...] = x * lax.rsqrt(ms + EPS) * g_ref[...]


def rmsnorm_rows(x, g, *, tm, name):
    M, D = x.shape
    tm = min(tm, M)
    return pl.pallas_call(
        _rmsnorm_kernel,
        grid=(M // tm,),
        in_specs=[pl.BlockSpec((tm, D), lambda i: (i, 0)),
                  pl.BlockSpec((1, D), lambda i: (0, 0))],
        out_specs=pl.BlockSpec((tm, D), lambda i: (i, 0)),
        out_shape=jax.ShapeDtypeStruct((M, D), F32),
        compiler_params=_params("parallel"),
        name=name,
    )(x, g.reshape(1, D))


def _layernorm(v, g, b):
    mu = jnp.mean(v, axis=-1, keepdims=True)
    vc = v - mu
    var = jnp.mean(vc * vc, axis=-1, keepdims=True)
    return vc * lax.rsqrt(var + EPS) * g + b


def _sgu_kernel(u_ref, v_ref, z_ref, lng_ref, lnb_ref, w_ref, bt_ref, o_ref):
    rows, width = u_ref.shape
    ch = width // SGU_GROUPS
    r = lax.broadcasted_iota(jnp.int32, (CHUNK, CHUNK), 0)
    c = lax.broadcasted_iota(jnp.int32, (CHUNK, CHUNK), 1)
    causal = c <= r
    ws = [jnp.where(causal, w_ref[g], 0.0).astype(BF16) for g in range(SGU_GROUPS)]
    for n in range(rows // CHUNK):
        rs = slice(n * CHUNK, (n + 1) * CHUNK)
        vn = _layernorm(jax.nn.gelu(v_ref[rs, :]), lng_ref[...], lnb_ref[...]).astype(BF16)
        for g in range(SGU_GROUPS):
            cs = slice(g * ch, (g + 1) * ch)
            mixed = jnp.dot(ws[g], vn[:, cs], preferred_element_type=F32) + bt_ref[:, g:g + 1]
            u = jax.nn.gelu(u_ref[rs, cs])
            o_ref[rs, cs] = (u * mixed * jax.nn.silu(z_ref[rs, cs])).astype(o_ref.dtype)


def sgu_prompt(p, ln_g, ln_b, w_s, b_s, *, width, rows, name):
    M = p.shape[0]
    return pl.pallas_call(
        _sgu_kernel,
        grid=(M // rows,),
        in_specs=[pl.BlockSpec((rows, width), lambda i: (i, 0)),
                  pl.BlockSpec((rows, width), lambda i: (i, 1)),
                  pl.BlockSpec((rows, width), lambda i: (i, 2)),
                  pl.BlockSpec((1, width), lambda i: (0, 0)),
                  pl.BlockSpec((1, width), lambda i: (0, 0)),
                  pl.BlockSpec((SGU_GROUPS, CHUNK, CHUNK), lambda i: (0, 0, 0)),
                  pl.BlockSpec((CHUNK, SGU_GROUPS), lambda i: (0, 0))],
        out_specs=pl.BlockSpec((rows, width), lambda i: (i, 0)),
        out_shape=jax.ShapeDtypeStruct((M, width), BF16),
        compiler_params=_params("parallel"),
        name=name,
    )(p, p, p, ln_g.reshape(1, width), ln_b.reshape(1, width), w_s, b_s.T)


def _sgu_first_row_kernel(u_ref, v_ref, z_ref, lng_ref, lnb_ref, w0_ref, b0_ref, o_ref, vn_ref):
    vn = _layernorm(jax.nn.gelu(v_ref[...]), lng_ref[...], lnb_ref[...])
    vn_ref[...] = vn
    mixed = w0_ref[...] * vn + b0_ref[...]
    o_ref[...] = (jax.nn.gelu(u_ref[...]) * mixed * jax.nn.silu(z_ref[...])).astype(o_ref.dtype)


def sgu_first_row(p, ln_g, ln_b, w_s, b_s, *, width, name):
    M = p.shape[0]
    ch = width // SGU_GROUPS
    w0 = jnp.repeat(w_s[:, 0, 0], ch).reshape(1, width)
    b0 = jnp.repeat(b_s[:, 0], ch).reshape(1, width)
    row = pl.BlockSpec((1, width), lambda i: (0, 0))
    return pl.pallas_call(
        _sgu_first_row_kernel,
        grid=(1,),
        in_specs=[pl.BlockSpec((M, width), lambda i: (0, 0)),
                  pl.BlockSpec((M, width), lambda i: (0, 1)),
                  pl.BlockSpec((M, width), lambda i: (0, 2)),
                  row, row, row, row],
        out_specs=[pl.BlockSpec((M, width), lambda i: (0, 0)),
                   pl.BlockSpec((M, width), lambda i: (0, 0))],
        out_shape=[jax.ShapeDtypeStruct((M, width), BF16),
                   jax.ShapeDtypeStruct((M, width), F32)],
        compiler_params=_params("arbitrary"),
        name=name,
    )(p, p, p, ln_g.reshape(1, width), ln_b.reshape(1, width), w0, b0)


def _mem_attn_kernel(q_ref, zm_ref, k_ref, v_ref, o_ref):
    tm, width = q_ref.shape
    hd = width // MEM_HEADS
    q = q_ref[...]
    if tm < 8:
        q = jnp.broadcast_to(q[0:1], (8, width))
    for h in range(MEM_HEADS):
        cs = slice(h * hd, (h + 1) * hd)
        s = lax.dot_general(q[:, cs].astype(BF16), k_ref[:, cs].astype(BF16), NT_DIMS,
                            preferred_element_type=F32) * (hd ** -0.5)
        e = jnp.exp(s - jnp.max(s, axis=-1, keepdims=True))
        p = e * (1.0 / jnp.sum(e, axis=-1, keepdims=True))
        o = jnp.dot(p.astype(BF16), v_ref[:, cs].astype(BF16), preferred_element_type=F32)
        o_ref[:, cs] = (o[:tm] * jax.nn.silu(zm_ref[:, cs])).astype(o_ref.dtype)


def mem_attention(p, mem_k, mem_v, *, lead_k, lead_v, q_block, tm, name):
    B, L, _ = p.shape
    mlen, width = mem_k.shape[-2:]
    tm = min(tm, L)
    mem_spec = lambda lead: pl.BlockSpec((None,) * (len(lead) + 1) + (mlen, width),
                                         lambda b, i: (*lead, b, 0, 0))
    return pl.pallas_call(
        _mem_attn_kernel,
        grid=(B, L // tm),
        in_specs=[pl.BlockSpec((None, tm, width), lambda b, i: (b, i, q_block)),
                  pl.BlockSpec((None, tm, width), lambda b, i: (b, i, q_block + 1)),
                  mem_spec(lead_k), mem_spec(lead_v)],
        out_specs=pl.BlockSpec((None, tm, width), lambda b, i: (b, i, 0)),
        out_shape=jax.ShapeDtypeStruct((B, L, width), BF16),
        compiler_params=_params("parallel", "parallel"),
        name=name,
    )(p, p, mem_k, mem_v)


def _rel_bucket(dist):
    n = jnp.maximum(dist, 0)
    max_exact = N_BUCKETS // 2
    nf = jnp.maximum(n, 1).astype(F32)
    large = max_exact + (jnp.log(nf / max_exact) / math.log(MAX_DISTANCE / max_exact)
                         * (N_BUCKETS - max_exact)).astype(jnp.int32)
    large = jnp.minimum(large, N_BUCKETS - 1)
    return jnp.where(n < max_exact, n, large)


def _lambda(lam_ref, lam_init):
    a = jnp.sum(lam_ref[0:1, :] * lam_ref[1:2, :], axis=-1, keepdims=True)
    b = jnp.sum(lam_ref[2:3, :] * lam_ref[3:4, :], axis=-1, keepdims=True)
    return jnp.exp(a) - jnp.exp(b) + lam_init


def _subln_gate(o, g, z, lam_init):
    ms = jnp.mean(o * o, axis=-1, keepdims=True)
    on = (o * lax.rsqrt(ms + EPS) * g) * (1.0 - lam_init)
    return on * jax.nn.silu(z)


def _lane_groups(x):
    return [x[:, g * 128:(g + 1) * 128] for g in range(x.shape[1] // 128)]


def _diff_attn_kernel(q_ref, z_ref, k_ref, v_ref, bd_ref, be_ref, bfar_ref, lam_ref, g_ref, o_ref,
                      vb_ref, b0_ref, b1_ref, s_ref, mp_ref, lp_ref, acc_ref, *, lam_init):
    t = q_ref.shape[0]
    qi = pl.program_id(2)
    scale = B_HD ** -0.5
    md = MAX_DISTANCE
    nb = t // md

    @pl.when(qi == 0)
    def _():
        vb_ref[...] = v_ref[...].astype(BF16)
        far = jnp.broadcast_to(bfar_ref[0:1, 0:1], (md, md))
        masked = jnp.full((md, md), NEG, F32)
        near, next_to_near = bd_ref[...], be_ref[...]
        for bi in range(nb):
            for bj in range(nb):
                piece = near if bi == bj else next_to_near if bi == bj + 1 else far if bi > bj else masked
                b0_ref[bi * md:(bi + 1) * md, bj * md:(bj + 1) * md] = piece
                piece = next_to_near if (bi == 0 and bj == nb - 1) else far
                b1_ref[bi * md:(bi + 1) * md, bj * md:(bj + 1) * md] = piece

    q = q_ref[...]
    qs = (q[:, :B_HD].astype(BF16), q[:, B_HD:].astype(BF16))
    mp_ref[...] = jnp.full(mp_ref.shape, NEG, F32)
    lp_ref[...] = jnp.zeros(lp_ref.shape, F32)
    acc_ref[...] = jnp.zeros(acc_ref.shape, F32)

    def scores(kj, bias):
        off = pl.multiple_of(kj * t, t)
        k = k_ref[pl.ds(off, t), :]
        for c in range(2):
            s = lax.dot_general(qs[c], k[:, c * B_HD:(c + 1) * B_HD], NT_DIMS,
                                preferred_element_type=F32) * scale + bias
            s_ref[c, :, pl.ds(off, t)] = s
            mp_ref[c] = functools.reduce(jnp.maximum, _lane_groups(s), mp_ref[c])

    bfar = bfar_ref[0:1, 0:1]

    def far(kj, carry):
        scores(kj, bfar)
        return carry

    lax.fori_loop(0, jnp.maximum(qi - 1, 0), far, 0)

    @pl.when(qi >= 1)
    def _():
        scores(qi - 1, b1_ref[...])

    scores(qi, b0_ref[...])

    m_rows = [jnp.broadcast_to(jnp.max(mp_ref[c], axis=-1, keepdims=True), (t, 128)) for c in range(2)]

    def accumulate(kj, carry):
        off = pl.multiple_of(kj * t, t)
        v = vb_ref[pl.ds(off, t), :]
        for c in range(2):
            ps = [jnp.exp(sg - m_rows[c]) for sg in _lane_groups(s_ref[c, :, pl.ds(off, t)])]
            lp_ref[c] = functools.reduce(jnp.add, ps, lp_ref[c])
            p = jnp.concatenate(ps, axis=1).astype(BF16)
            acc_ref[c] = acc_ref[c] + jnp.dot(p, v, preferred_element_type=F32)
        return carry

    lax.fori_loop(0, qi + 1, accumulate, 0)

    lam = _lambda(lam_ref, lam_init)
    inv_l = [1.0 / jnp.sum(lp_ref[c], axis=-1, keepdims=True) for c in range(2)]
    o = acc_ref[0] * inv_l[0] - lam * (acc_ref[1] * inv_l[1])
    o_ref[...] = _subln_gate(o, g_ref[...], z_ref[...], lam_init).astype(o_ref.dtype)


def _toeplitz_tiles(g, t):
    H, P = g.shape
    hankel = jnp.tile(g[:, ::-1], (1, t + 1))[:, :t * (P + 1)].reshape(H, t, P + 1)
    return hankel[:, ::-1, :t]


def diff_attention_prompt(p, k, v, rel_bias, lam_rows, subln_g, *, lam_init, t, name):
    B, L, _ = p.shape
    H = k.shape[1]
    md = MAX_DISTANCE
    assert t % md == 0 and L % t == 0
    table = rel_bias.astype(F32)
    d = jnp.arange(-(md - 1), md, dtype=jnp.int32)
    bd = _toeplitz_tiles(jnp.where((d >= 0)[None], table[_rel_bucket(d)].T, NEG), md)
    be = _toeplitz_tiles(table[_rel_bucket(d + md)].T, md)
    bfar = jnp.broadcast_to(table[_rel_bucket(jnp.int32(md + 1))][:, None, None], (H, 1, 128))
    blk = lambda off: pl.BlockSpec((None, t, B_HD2), lambda b, h, i: (b, i, off + h))
    kv = pl.BlockSpec((None, None, L, B_HD2), lambda b, h, i: (b, h, 0, 0))
    tile = pl.BlockSpec((None, md, md), lambda b, h, i: (h, 0, 0))
    return pl.pallas_call(
        functools.partial(_diff_attn_kernel, lam_init=lam_init),
        grid=(B, H, L // t),
        in_specs=[blk(0), blk(H), kv, kv, tile, tile,
                  pl.BlockSpec((None, 1, 128), lambda b, h, i: (h, 0, 0)),
                  pl.BlockSpec((4, B_HD), lambda b, h, i: (0, 0)),
                  pl.BlockSpec((1, B_HD2), lambda b, h, i: (0, 0))],
        out_specs=pl.BlockSpec((None, t, B_HD2), lambda b, h, i: (b, i, h)),
        out_shape=jax.ShapeDtypeStruct((B, L, H * B_HD2), BF16),
        scratch_shapes=[pltpu.VMEM((L, B_HD2), BF16), pltpu.VMEM((t, t), F32), pltpu.VMEM((t, t), F32),
                        pltpu.VMEM((2, t, L), F32), pltpu.VMEM((2, t, 128), F32), pltpu.VMEM((2, t, 128), F32),
                        pltpu.VMEM((2, t, B_HD2), F32)],
        compiler_params=_params("parallel", "parallel", "arbitrary"),
        name=name,
    )(p, p, k, v, bd, be, bfar, lam_rows, subln_g.reshape(1, B_HD2))


def _in_proj_decode_kernel(pt_ref, x_ref, g_ref, w_ref, ps_ref, kn_ref, vn_ref, sb_ref, lam_ref, sg_ref, *rest,
                           pps, lam_init, steps_per_seq):
    del pt_ref
    k_pages, v_pages = rest[:pps], rest[pps:2 * pps]
    o_ref, d_ref, h_ref, qb_ref, m_ref, l_ref, acc_ref = rest[2 * pps:]
    H, rows, _ = qb_ref.shape
    width = H * B_HD2
    step = pl.program_id(0) * pl.num_programs(1) + pl.program_id(1)
    b = step // steps_per_seq
    first = step % steps_per_seq == 0
    last = step % steps_per_seq == steps_per_seq - 1
    scale = B_HD ** -0.5

    def query_rows(hd):
        r_id = lax.broadcasted_iota(jnp.int32, (rows, B_HD2), 0)
        c_id = lax.broadcasted_iota(jnp.int32, (rows, B_HD2), 1)
        own = (c_id >> 7) == r_id
        return jnp.where(own, ps_ref[pl.ds(b, 1), hd * B_HD2:(hd + 1) * B_HD2], 0.0)

    _normalised_rows(x_ref, g_ref, h_ref)

    @pl.when(first)
    def _():
        for hd in range(H):
            qb_ref[hd] = query_rows(hd).astype(BF16)
        m_ref[...] = jnp.full(m_ref.shape, NEG, F32)
        l_ref[...] = jnp.zeros(l_ref.shape, F32)
        acc_ref[...] = jnp.zeros(acc_ref.shape, F32)

    def softmax_step(hd, s):
        m_old = m_ref[hd]
        m_new = jnp.maximum(m_old, jnp.max(s, axis=-1, keepdims=True))
        alpha = jnp.exp(m_old - m_new)
        p = jnp.exp(s - m_new)
        l_ref[hd] = alpha * l_ref[hd] + jnp.sum(p, axis=-1, keepdims=True)
        m_ref[hd] = m_new
        return alpha, p

    def head_scores(hd):
        kh = jnp.concatenate(
            [jnp.concatenate([kp[pl.ds(2 * hd + c, PAGE_SIZE, stride=2 * H), :] for c in range(2)], axis=1)
             for kp in k_pages], axis=0).astype(BF16)
        bias = jnp.concatenate([sb_ref[0, hd:hd + 1, :]] * (pps - 1)
                               + [jnp.where(last, sb_ref[1, hd:hd + 1, :], sb_ref[0, hd:hd + 1, :])], axis=1)
        return lax.dot_general(qb_ref[hd], kh, NT_DIMS, preferred_element_type=F32) * scale + bias

    def head_values(hd, alpha, p):
        vh = jnp.concatenate([vp[hd] for vp in v_pages], axis=0).astype(BF16)
        acc_ref[hd] = alpha * acc_ref[hd] + jnp.dot(p.astype(BF16), vh, preferred_element_type=F32)

    kc = min(MXU_DEPTH_V7X, x_ref.shape[1])
    n_chunks = x_ref.shape[1] // kc
    lag = 4
    folded = {}
    acc = None
    for c in range(n_chunks):
        part = jnp.dot(h_ref[:, c * kc:(c + 1) * kc], w_ref[c * kc:(c + 1) * kc, :], preferred_element_type=F32)
        acc = part if acc is None else acc + part
        for hd in range(H):
            if hd * n_chunks // (H + lag) == c:
                folded[hd] = softmax_step(hd, head_scores(hd))
        for hd in range(H):
            if (hd + lag) * n_chunks // (H + lag) == c:
                head_values(hd, *folded[hd])
    o_ref[...] = acc

    @pl.when(last)
    def _():
        lam = _lambda(lam_ref, lam_init)
        for hd in range(H):
            cs = slice(hd * B_HD2, (hd + 1) * B_HD2)
            s = (jnp.sum(query_rows(hd) * kn_ref[pl.ds(b, 1), cs], axis=-1, keepdims=True) * scale
                 + sb_ref[2, hd:hd + 1, 0:1])
            alpha, p = softmax_step(hd, s)
            o_both = (alpha * acc_ref[hd] + p * vn_ref[pl.ds(b, 1), cs]) * (1.0 / l_ref[hd])
            o = o_both[0:1] - lam * o_both[1:2]
            z = ps_ref[pl.ds(b, 1), width + hd * B_HD2:width + (hd + 1) * B_HD2]
            d_ref[pl.ds(b, 1), cs] = _subln_gate(o, sg_ref[...], z, lam_init)


def in_proj_with_decode(x, g, w, p_s, k_new, v_new, cache_k, cache_v, page_table, rel_bias, lam_rows, subln_g,
                        *, lam_init, tm, tn, name):
    M, D = x.shape
    N = w.shape[1]
    B, n_pages = page_table.shape
    width = k_new.shape[-1]
    H = width // B_HD2
    k_rows = PAGE_SIZE * 2 * H
    ni, nj = M // tm, N // tn
    assert (B * n_pages) % (ni * nj) == 0, "every grid step takes the same number of pages"
    pps = B * n_pages // (ni * nj)
    assert n_pages % pps == 0 and PAGE_SIZE >= MAX_DISTANCE
    rows = 8
    table = rel_bias.astype(F32)
    dist_last = PAGE_SIZE - jnp.arange(PAGE_SIZE, dtype=jnp.int32)
    sb = jnp.stack([
        jnp.broadcast_to(table[_rel_bucket(jnp.int32(PAGE_SIZE + 1))][:, None], (H, PAGE_SIZE)),
        table[_rel_bucket(dist_last)].T,
        jnp.broadcast_to(table[_rel_bucket(jnp.int32(0))][:, None], (H, PAGE_SIZE)),
    ])
    sg = subln_g.reshape(1, B_HD2)
    whole = lambda a: pl.BlockSpec(a.shape, lambda i, j, pt: (0,) * a.ndim)
    k_page = lambda r: pl.BlockSpec((k_rows, B_HD), lambda i, j, pt: (pt[(i * nj + j) * pps + r], 0))
    v_page = lambda r: pl.BlockSpec((None, H, PAGE_SIZE, B_HD2),
                                    lambda i, j, pt: (pt[(i * nj + j) * pps + r], 0, 0, 0))
    grid_spec = pltpu.PrefetchScalarGridSpec(
        num_scalar_prefetch=1,
        grid=(ni, nj),
        in_specs=[pl.BlockSpec((tm, D), lambda i, j, pt: (i, 0)),
                  pl.BlockSpec((1, D), lambda i, j, pt: (0, 0)),
                  pl.BlockSpec((D, tn), lambda i, j, pt: (0, j)),
                  whole(p_s), whole(k_new), whole(v_new), whole(sb), whole(lam_rows), whole(sg)]
                 + [k_page(r) for r in range(pps)] + [v_page(r) for r in range(pps)],
        out_specs=[pl.BlockSpec((tm, tn), lambda i, j, pt: (i, j)),
                   pl.BlockSpec((B, width), lambda i, j, pt: (0, 0))],
        scratch_shapes=[pltpu.VMEM((tm, D), BF16), pltpu.VMEM((H, rows, B_HD2), BF16),
                        pltpu.VMEM((H, rows, 1), F32), pltpu.VMEM((H, rows, 1), F32),
                        pltpu.VMEM((H, rows, B_HD2), F32)],
    )
    return pl.pallas_call(
        functools.partial(_in_proj_decode_kernel, pps=pps, lam_init=lam_init, steps_per_seq=n_pages // pps),
        grid_spec=grid_spec,
        out_shape=[jax.ShapeDtypeStruct((M, N), F32), jax.ShapeDtypeStruct((B, width), F32)],
        compiler_params=_params("arbitrary", "arbitrary"),
        name=name,
    )(page_table.reshape(-1), x, g.reshape(1, D), w, p_s, k_new, v_new, sb, lam_rows, sg,
      *([cache_k] * pps), *([cache_v] * pps))


def _trunk(x_prompt, x_sample, mkv, smk, smv, past, W):
    bp, L, D = x_prompt.shape
    bs = x_sample.shape[0]
    mp = bp * L
    depth = W["norm_g"].shape[0]
    n_a = W["w_in_a"].shape[0]
    e_a = W["sgu_ln_g"].shape[1]
    b_width = W["w_kv"].shape[1] // 2
    mem_width = mkv.shape[-1]
    H = b_width // B_HD2
    tm, tn = 512, 1024
    tn_cast = 512
    xp = x_prompt.reshape(mp, D)
    xs = x_sample.reshape(bs, D)
    chunk_vs = []

    def mem_both(l, pp, ps, q_block):
        mo_p = mem_attention(pp.reshape(bp, L, -1), mkv, mkv, lead_k=(l, 0), lead_v=(l, 1),
                             q_block=q_block, tm=512, name=f"mem{l}_p")
        mo_s = mem_attention(ps.reshape(bs, 1, -1), smk, smv, lead_k=(l,), lead_v=(l,),
                             q_block=q_block, tm=1, name=f"mem{l}_s")
        return mo_p.reshape(mp, mem_width), mo_s.reshape(bs, mem_width)

    for l in range(depth):
        g_in = W["norm_g"][l]
        if l < n_a:
            ps, wb = rms_matmul_cast(xs, g_in, W["w_in_a"], layer=l, tn=tn_cast, name=f"in_a{l}_s")
            pp = rms_matmul(xp, g_in, wb, tm=tm, tn=tn, name=f"in_a{l}_p")
            sgu_w = (W["sgu_ln_g"][l], W["sgu_ln_b"][l], W["sgu_w"][l], W["sgu_b"][l])
            main_s, vn = sgu_first_row(ps, *sgu_w, width=e_a, name=f"sgu{l}_s")
            chunk_vs.append(vn)
            main_p = sgu_prompt(pp, *sgu_w, width=e_a, rows=2 * CHUNK, name=f"sgu{l}_p")
            q_block = 3 * e_a // mem_width
            w_out, w_layer = W["w_out_a"], l
        else:
            jb = l - n_a
            lam_init = 0.8 - 0.6 * math.exp(-0.3 * l)
            lam_rows = jnp.stack([W["lambda_q1"][jb], W["lambda_k1"][jb], W["lambda_q2"][jb], W["lambda_k2"][jb]])
            ps, wb = rms_matmul_cast(xs, g_in, W["w_in_b"], layer=jb, tn=tn_cast, name=f"in_b{jb}_s")
            pp, main_s = in_proj_with_decode(xp, g_in, wb, ps, k_s, v_s, *past, W["rel_bias"], lam_rows,
                                             W["subln_g"][jb], lam_init=lam_init, tm=tm, tn=tn_cast,
                                             name=f"in_b{jb}_p_dattn{jb}_s")
            main_s = main_s.astype(BF16)
            main_p = diff_attention_prompt(pp.reshape(bp, L, -1), k_att, v_p, W["rel_bias"], lam_rows,
                                           W["subln_g"][jb], lam_init=lam_init, t=512, name=f"dattn{jb}_p")
            main_p = main_p.reshape(mp, b_width)
            q_block = 2 * b_width // mem_width
            w_out, w_layer = W["w_out_b"], jb
        mo_p, mo_s = mem_both(l, pp, ps, q_block)
        xs, wb = out_proj_cast(main_s, mo_s, w_out, xs, layer=w_layer, tn=tn_cast, name=f"out{l}_s")
        if l < depth - 1:
            xp = out_proj(main_p, mo_p, wb, xp, tm=tm, tn=tn, name=f"out{l}_p")
        else:
            y_p = out_proj_norm(main_p, mo_p, wb, xp, W["final_norm_g"], tm=tm, tn=tn, name=f"out{l}_p")
        if l == n_a - 1:
            g_kv = W["kv_norm_g"]
            k_s, wb_k = rms_matmul_cast(xs, g_kv, W["w_kv"], col0=0, n_cols=b_width, tn=tn_cast, name="k_s")
            v_s, wb_v = rms_matmul_cast(xs, g_kv, W["w_kv"], col0=b_width, n_cols=b_width, tn=tn_cast, name="v_s")
            k_p, k_att = k_proj_prompt(xp, g_kv, wb_k, B=bp, tm=tm, heads=2, name="k_p")
            v_p = v_proj_prompt(xp, g_kv, wb_v, B=bp, tm=tm, heads=4, name="v_p")
    y_s = rmsnorm_rows(xs, W["final_norm_g"], tm=bs, name="final_s")
    return (y_p.reshape(bp, L, D), y_s.reshape(bs, 1, D),
            k_p.reshape(bp, L, H, 2, B_HD), jnp.transpose(v_p, (0, 2, 1, 3)),
            k_s.reshape(bs, 1, H, 2, B_HD), v_s.reshape(bs, 1, H, B_HD2), chunk_vs)


def kernel(x_prompt, x_sample, cache_k, cache_v, cache_mem_k, cache_mem_v, page_table, mem_prompt,
           norm_g, final_norm_g, mem_norm_g, w_mem_kv, w_in_a, sgu_ln_g, sgu_ln_b, sgu_w, sgu_b,
           w_out_a, kv_norm_g, w_kv, w_in_b, lambda_q1, lambda_k1, lambda_q2, lambda_k2,
           subln_g, rel_bias, w_out_b):
    depth = norm_g.shape[0]
    b_width = w_kv.shape[1] // 2
    mem_width = w_mem_kv.shape[-1] // 2
    mem_hd = mem_width // MEM_HEADS
    W = dict(norm_g=norm_g, final_norm_g=final_norm_g, sgu_ln_g=sgu_ln_g, sgu_ln_b=sgu_ln_b, sgu_w=sgu_w,
             sgu_b=sgu_b, kv_norm_g=kv_norm_g, lambda_q1=lambda_q1, lambda_k1=lambda_k1, lambda_q2=lambda_q2,
             lambda_k2=lambda_k2, subln_g=subln_g, rel_bias=rel_bias,
             w_in_a=w_in_a, w_out_a=w_out_a, w_in_b=w_in_b, w_out_b=w_out_b, w_kv=w_kv)

    bd, dec_seq = x_sample.shape[:2]
    assert dec_seq == 1, "the sample group is one new token per sequence"
    past = (cache_k.reshape(-1, B_HD), jnp.transpose(cache_v, (0, 2, 1, 3)), page_table)
    smk = cache_mem_k.reshape(depth, bd, -1, mem_width)
    smv = cache_mem_v.reshape(depth, bd, -1, mem_width)

    bp, m_len, d_model = mem_prompt.shape
    mkv = rms_matmul_stacked(mem_prompt.reshape(bp * m_len, d_model), mem_norm_g, w_mem_kv.astype(BF16),
                             tm=512, tn=mem_width, name="mem_kv")
    mkv = mkv.reshape(depth, 2, bp, m_len, mem_width)

    y_p, y_s, k_p, v_p, k_s, v_s, chunk_vs = _trunk(x_prompt, x_sample, mkv, smk, smv, past, W)
    chunk_v_s = jnp.stack(chunk_vs).reshape(len(chunk_vs), bd, dec_seq, -1)
    mem_shape = (depth, bp, m_len, MEM_HEADS, mem_hd)
    return (y_p, y_s, k_p, v_p, k_s, v_s, mkv[:, 0].reshape(mem_shape), mkv[:, 1].reshape(mem_shape), chunk_v_s)
```

```python
import functools
import math

import jax
import jax.numpy as jnp
from jax import lax
from jax.experimental import pallas as pl
from jax.experimental.pallas import tpu as pltpu

F32 = jnp.float32
BF16 = jnp.bfloat16

CHUNK = 128
SGU_GROUPS = 8
B_HD2 = 256
B_HD = B_HD2 // 2
MEM_HEADS = 4
N_BUCKETS = 32
MAX_DISTANCE = 128
PAGE_SIZE = 128
EPS = 1e-6
NEG = -1e30

VMEM_BYTES_V7X = 64 * 1024 * 1024
MXU_DEPTH_V7X = 256
VMEM_LIMIT = VMEM_BYTES_V7X - 8 * 1024 * 1024

NT_DIMS = (((1,), (1,)), ((), ()))


def _params(*sem):
    return pltpu.CompilerParams(dimension_semantics=sem, vmem_limit_bytes=VMEM_LIMIT)


def _normalised_rows(x_ref, g_ref, h_ref):
    @pl.when(pl.program_id(1) == 0)
    def _():
        x = x_ref[...]
        ms = jnp.mean(x * x, axis=-1, keepdims=True)
        h_ref[...] = (x * lax.rsqrt(ms + EPS) * g_ref[...]).astype(BF16)

    return h_ref[...]


def _rms_dot(x_ref, g_ref, w_ref, h_ref, emit):
    first = pl.program_id(1) == 0

    @pl.when(first)
    def _():
        x = x_ref[...]
        r = lax.rsqrt(jnp.mean(x * x, axis=-1, keepdims=True) + EPS)
        kc = min(MXU_DEPTH_V7X, x_ref.shape[1])
        acc = None
        for c in range(x_ref.shape[1] // kc):
            cs = slice(c * kc, (c + 1) * kc)
            hc = (x_ref[:, cs] * r * g_ref[:, cs]).astype(BF16)
            h_ref[:, cs] = hc
            part = jnp.dot(hc, w_ref[cs, :], preferred_element_type=F32)
            acc = part if acc is None else acc + part
        emit(acc)

    @pl.when(jnp.logical_not(first))
    def _():
        emit(jnp.dot(h_ref[...], w_ref[...], preferred_element_type=F32))


def _rms_matmul_kernel(x_ref, g_ref, w_ref, o_ref, h_ref):
    def emit(res):
        o_ref[...] = res

    _rms_dot(x_ref, g_ref, w_ref, h_ref, emit)


def rms_matmul(x, g, w, *, tm, tn, name):
    M, D = x.shape
    N = w.shape[1]
    return pl.pallas_call(
        _rms_matmul_kernel,
        grid=(M // tm, N // tn),
        in_specs=[pl.BlockSpec((tm, D), lambda i, j: (i, 0)),
                  pl.BlockSpec((1, D), lambda i, j: (0, 0)),
                  pl.BlockSpec((D, tn), lambda i, j: (0, j))],
        out_specs=pl.BlockSpec((tm, tn), lambda i, j: (i, j)),
        out_shape=jax.ShapeDtypeStruct((M, N), F32),
        scratch_shapes=[pltpu.VMEM((tm, D), BF16)],
        compiler_params=_params("parallel", "arbitrary"),
        name=name,
    )(x, g.reshape(1, D), w)


def _rms_matmul_cast_kernel(x_ref, g_ref, w_ref, o_ref, wb_ref, h_ref):
    h = _normalised_rows(x_ref, g_ref, h_ref)
    wb = w_ref[...].astype(BF16)
    wb_ref[...] = wb
    o_ref[...] = jnp.dot(h, wb, preferred_element_type=F32)


def _f32_weight_spec(w, layer, tn, first_block=0):
    if w.ndim == 2:
        return pl.BlockSpec((w.shape[0], tn), lambda i, j: (0, first_block + j))
    return pl.BlockSpec((None, w.shape[1], tn), lambda i, j: (layer, 0, first_block + j))


def rms_matmul_cast(x, g, w, *, tn, name, layer=0, col0=0, n_cols=None, column_blocked=False):
    M, D = x.shape
    n_cols = w.shape[-1] if n_cols is None else n_cols
    if column_blocked:
        wb_spec = pl.BlockSpec((None, D, tn), lambda i, j: (j, 0, 0))
        wb_shape = jax.ShapeDtypeStruct((n_cols // tn, D, tn), BF16)
    else:
        wb_spec = pl.BlockSpec((D, tn), lambda i, j: (0, j))
        wb_shape = jax.ShapeDtypeStruct((D, n_cols), BF16)
    return pl.pallas_call(
        _rms_matmul_cast_kernel,
        grid=(1, n_cols // tn),
        in_specs=[pl.BlockSpec((M, D), lambda i, j: (0, 0)),
                  pl.BlockSpec((1, D), lambda i, j: (0, 0)),
                  _f32_weight_spec(w, layer, tn, col0 // tn)],
        out_specs=[pl.BlockSpec((M, tn), lambda i, j: (0, j)), wb_spec],
        out_shape=[jax.ShapeDtypeStruct((M, n_cols), F32), wb_shape],
        scratch_shapes=[pltpu.VMEM((M, D), BF16)],
        compiler_params=_params("arbitrary", "arbitrary"),
        name=name,
    )(x, g.reshape(1, D), w)


def rms_matmul_stacked(x, g, w, *, tm, tn, name):
    M, D = x.shape
    G, _, N = w.shape
    nb = N // tn
    return pl.pallas_call(
        _rms_matmul_kernel,
        grid=(M // tm, G * nb),
        in_specs=[pl.BlockSpec((tm, D), lambda i, j: (i, 0)),
                  pl.BlockSpec((1, D), lambda i, j: (0, 0)),
                  pl.BlockSpec((None, D, tn), lambda i, j: (j // nb, 0, j % nb))],
        out_specs=pl.BlockSpec((None, tm, tn), lambda i, j: (j, i, 0)),
        out_shape=jax.ShapeDtypeStruct((G * nb, M, tn), F32),
        scratch_shapes=[pltpu.VMEM((tm, D), BF16)],
        compiler_params=_params("parallel", "arbitrary"),
        name=name,
    )(x, g.reshape(1, D), w)


def _k_proj_kernel(x_ref, g_ref, w_ref, knat_ref, katt_ref, h_ref):
    tm = x_ref.shape[0]
    heads = katt_ref.shape[0]
    j = pl.program_id(1)
    n_sub = knat_ref.shape[0] // tm

    def emit(res):
        for q in range(2 * heads):
            knat_ref[pl.ds(j * 2 * heads + q, tm, stride=n_sub), :] = res[:, q * B_HD:(q + 1) * B_HD]
        for hh in range(heads):
            katt_ref[hh] = res[:, hh * B_HD2:(hh + 1) * B_HD2].astype(BF16)

    _rms_dot(x_ref, g_ref, w_ref, h_ref, emit)


def k_proj_prompt(x, g, w, *, B, tm, heads, name):
    M, D = x.shape
    L = M // B
    N = w.shape[1]
    H = N // B_HD2
    nlb = L // tm
    return pl.pallas_call(
        _k_proj_kernel,
        grid=(M // tm, H // heads),
        in_specs=[pl.BlockSpec((tm, D), lambda i, j: (i, 0)),
                  pl.BlockSpec((1, D), lambda i, j: (0, 0)),
                  pl.BlockSpec((D, heads * B_HD2), lambda i, j: (0, j))],
        out_specs=[pl.BlockSpec((tm * 2 * H, B_HD), lambda i, j: (i, 0)),
                   pl.BlockSpec((None, heads, tm, B_HD2), lambda i, j: (i // nlb, j, i % nlb, 0))],
        out_shape=[jax.ShapeDtypeStruct((M * 2 * H, B_HD), F32),
                   jax.ShapeDtypeStruct((B, H, L, B_HD2), BF16)],
        scratch_shapes=[pltpu.VMEM((tm, D), BF16)],
        compiler_params=_params("parallel", "arbitrary"),
        name=name,
    )(x, g.reshape(1, D), w)


def _v_proj_kernel(x_ref, g_ref, w_ref, v_ref, h_ref):
    def emit(res):
        for hh in range(v_ref.shape[0]):
            v_ref[hh] = res[:, hh * B_HD2:(hh + 1) * B_HD2]

    _rms_dot(x_ref, g_ref, w_ref, h_ref, emit)


def v_proj_prompt(x, g, w, *, B, tm, heads, name):
    M, D = x.shape
    L = M // B
    H = w.shape[1] // B_HD2
    nlb = L // tm
    return pl.pallas_call(
        _v_proj_kernel,
        grid=(M // tm, H // heads),
        in_specs=[pl.BlockSpec((tm, D), lambda i, j: (i, 0)),
                  pl.BlockSpec((1, D), lambda i, j: (0, 0)),
                  pl.BlockSpec((D, heads * B_HD2), lambda i, j: (0, j))],
        out_specs=pl.BlockSpec((None, heads, tm, B_HD2), lambda i, j: (i // nlb, j, i % nlb, 0)),
        out_shape=jax.ShapeDtypeStruct((B, H, L, B_HD2), F32),
        scratch_shapes=[pltpu.VMEM((tm, D), BF16)],
        compiler_params=_params("parallel", "arbitrary"),
        name=name,
    )(x, g.reshape(1, D), w)


def _out_proj_kernel(a_ref, m_ref, w_ref, x_ref, o_ref):
    ka = a_ref.shape[1]
    acc = jnp.dot(a_ref[...], w_ref[:ka, :], preferred_element_type=F32)
    acc = acc + jnp.dot(m_ref[...], w_ref[ka:, :], preferred_element_type=F32)
    o_ref[...] = x_ref[...] + acc


def out_proj(a, m, w, x, *, tm, tn, name):
    M, ka = a.shape
    km = m.shape[1]
    N = w.shape[1]
    return pl.pallas_call(
        _out_proj_kernel,
        grid=(M // tm, N // tn),
        in_specs=[pl.BlockSpec((tm, ka), lambda i, j: (i, 0)),
                  pl.BlockSpec((tm, km), lambda i, j: (i, 0)),
                  pl.BlockSpec((ka + km, tn), lambda i, j: (0, j)),
                  pl.BlockSpec((tm, tn), lambda i, j: (i, j))],
        out_specs=pl.BlockSpec((tm, tn), lambda i, j: (i, j)),
        out_shape=jax.ShapeDtypeStruct((M, N), F32),
        compiler_params=_params("parallel", "arbitrary"),
        name=name,
    )(a, m, w, x)


def _out_proj_cast_kernel(a_ref, m_ref, w_ref, x_ref, o_ref, wb_ref):
    ka = a_ref.shape[1]
    wb = w_ref[...].astype(BF16)
    wb_ref[...] = wb
    acc = jnp.dot(a_ref[...], wb[:ka], preferred_element_type=F32)
    acc = acc + jnp.dot(m_ref[...], wb[ka:], preferred_element_type=F32)
    o_ref[...] = x_ref[...] + acc


def out_proj_cast(a, m, w, x, *, tn, name, layer=0):
    M, ka = a.shape
    km = m.shape[1]
    N = w.shape[-1]
    return pl.pallas_call(
        _out_proj_cast_kernel,
        grid=(1, N // tn),
        in_specs=[pl.BlockSpec((M, ka), lambda i, j: (0, 0)),
                  pl.BlockSpec((M, km), lambda i, j: (0, 0)),
                  _f32_weight_spec(w, layer, tn),
                  pl.BlockSpec((M, tn), lambda i, j: (0, j))],
        out_specs=[pl.BlockSpec((M, tn), lambda i, j: (0, j)),
                   pl.BlockSpec((ka + km, tn), lambda i, j: (0, j))],
        out_shape=[jax.ShapeDtypeStruct((M, N), F32),
                   jax.ShapeDtypeStruct((ka + km, N), BF16)],
        compiler_params=_params("arbitrary", "arbitrary"),
        name=name,
    )(a, m, w, x)


def _out_proj_norm_kernel(a_ref, m_ref, w_ref, x_ref, g_ref, o_ref):
    j = pl.program_id(1)
    ka = a_ref.shape[1]
    tn = x_ref.shape[1]
    acc = jnp.dot(a_ref[...], w_ref[:ka, :], preferred_element_type=F32)
    acc = acc + jnp.dot(m_ref[...], w_ref[ka:, :], preferred_element_type=F32)
    o_ref[:, pl.ds(pl.multiple_of(j * tn, tn), tn)] = x_ref[...] + acc

    @pl.when(j == pl.num_programs(1) - 1)
    def _():
        y = o_ref[...]
        ms = jnp.mean(y * y, axis=-1, keepdims=True)
        o_ref[...] = y * lax.rsqrt(ms + EPS) * g_ref[...]


def out_proj_norm(a, m, w, x, g, *, tm, tn, name):
    M, ka = a.shape
    km = m.shape[1]
    N = w.shape[-1]
    return pl.pallas_call(
        _out_proj_norm_kernel,
        grid=(M // tm, N // tn),
        in_specs=[pl.BlockSpec((tm, ka), lambda i, j: (i, 0)),
                  pl.BlockSpec((tm, km), lambda i, j: (i, 0)),
                  pl.BlockSpec((ka + km, tn), lambda i, j: (0, j)),
                  pl.BlockSpec((tm, tn), lambda i, j: (i, j)),
                  pl.BlockSpec((1, N), lambda i, j: (0, 0))],
        out_specs=pl.BlockSpec((tm, N), lambda i, j: (i, 0)),
        out_shape=jax.ShapeDtypeStruct((M, N), F32),
        compiler_params=_params("parallel", "arbitrary"),
        name=name,
    )(a, m, w, x, g.reshape(1, N))


def _rmsnorm_kernel(x_ref, g_ref, o_ref):
    x = x_ref[...]
    ms = jnp.mean(x * x, axis=-1, keepdims=True)
    o_ref[...] = x * lax.rsqrt(ms + EPS) * g_ref[...]


def rmsnorm_rows(x, g, *, tm, name):
    M, D = x.shape
    tm = min(tm, M)
    return pl.pallas_call(
        _rmsnorm_kernel,
        grid=(M // tm,),
        in_specs=[pl.BlockSpec((tm, D), lambda i: (i, 0)),
                  pl.BlockSpec((1, D), lambda i: (0, 0))],
        out_specs=pl.BlockSpec((tm, D), lambda i: (i, 0)),
        out_shape=jax.ShapeDtypeStruct((M, D), F32),
        compiler_params=_params("parallel"),
        name=name,
    )(x, g.reshape(1, D))


def _layernorm(v, g, b):
    mu = jnp.mean(v, axis=-1, keepdims=True)
    vc = v - mu
    var = jnp.mean(vc * vc, axis=-1, keepdims=True)
    return vc * lax.rsqrt(var + EPS) * g + b


def _sgu_kernel(u_ref, v_ref, z_ref, lng_ref, lnb_ref, w_ref, bt_ref, o_ref):
    rows, width = u_ref.shape
    ch = width // SGU_GROUPS
    r = lax.broadcasted_iota(jnp.int32, (CHUNK, CHUNK), 0)
    c = lax.broadcasted_iota(jnp.int32, (CHUNK, CHUNK), 1)
    causal = c <= r
    ws = [jnp.where(causal, w_ref[g], 0.0).astype(BF16) for g in range(SGU_GROUPS)]
    for n in range(rows // CHUNK):
        rs = slice(n * CHUNK, (n + 1) * CHUNK)
        vn = _layernorm(jax.nn.gelu(v_ref[rs, :]), lng_ref[...], lnb_ref[...]).astype(BF16)
        for g in range(SGU_GROUPS):
            cs = slice(g * ch, (g + 1) * ch)
            mixed = jnp.dot(ws[g], vn[:, cs], preferred_element_type=F32) + bt_ref[:, g:g + 1]
            u = jax.nn.gelu(u_ref[rs, cs])
            o_ref[rs, cs] = (u * mixed * jax.nn.silu(z_ref[rs, cs])).astype(o_ref.dtype)


def sgu_prompt(p, ln_g, ln_b, w_s, b_s, *, width, rows, name):
    M = p.shape[0]
    return pl.pallas_call(
        _sgu_kernel,
        grid=(M // rows,),
        in_specs=[pl.BlockSpec((rows, width), lambda i: (i, 0)),
                  pl.BlockSpec((rows, width), lambda i: (i, 1)),
                  pl.BlockSpec((rows, width), lambda i: (i, 2)),
                  pl.BlockSpec((1, width), lambda i: (0, 0)),
                  pl.BlockSpec((1, width), lambda i: (0, 0)),
                  pl.BlockSpec((SGU_GROUPS, CHUNK, CHUNK), lambda i: (0, 0, 0)),
                  pl.BlockSpec((CHUNK, SGU_GROUPS), lambda i: (0, 0))],
        out_specs=pl.BlockSpec((rows, width), lambda i: (i, 0)),
        out_shape=jax.ShapeDtypeStruct((M, width), BF16),
        compiler_params=_params("parallel"),
        name=name,
    )(p, p, p, ln_g.reshape(1, width), ln_b.reshape(1, width), w_s, b_s.T)


def _sgu_first_row_kernel(u_ref, v_ref, z_ref, lng_ref, lnb_ref, w0_ref, b0_ref, o_ref, vn_ref):
    vn = _layernorm(jax.nn.gelu(v_ref[...]), lng_ref[...], lnb_ref[...])
    vn_ref[...] = vn
    mixed = w0_ref[...] * vn + b0_ref[...]
    o_ref[...] = (jax.nn.gelu(u_ref[...]) * mixed * jax.nn.silu(z_ref[...])).astype(o_ref.dtype)


def sgu_first_row(p, ln_g, ln_b, w_s, b_s, *, width, name):
    M = p.shape[0]
    ch = width // SGU_GROUPS
    w0 = jnp.repeat(w_s[:, 0, 0], ch).reshape(1, width)
    b0 = jnp.repeat(b_s[:, 0], ch).reshape(1, width)
    row = pl.BlockSpec((1, width), lambda i: (0, 0))
    return pl.pallas_call(
        _sgu_first_row_kernel,
        grid=(1,),
        in_specs=[pl.BlockSpec((M, width), lambda i: (0, 0)),
                  pl.BlockSpec((M, width), lambda i: (0, 1)),
                  pl.BlockSpec((M, width), lambda i: (0, 2)),
                  row, row, row, row],
        out_specs=[pl.BlockSpec((M, width), lambda i: (0, 0)),
                   pl.BlockSpec((M, width), lambda i: (0, 0))],
        out_shape=[jax.ShapeDtypeStruct((M, width), BF16),
                   jax.ShapeDtypeStruct((M, width), F32)],
        compiler_params=_params("arbitrary"),
        name=name,
    )(p, p, p, ln_g.reshape(1, width), ln_b.reshape(1, width), w0, b0)


def _mem_attn_kernel(q_ref, zm_ref, k_ref, v_ref, o_ref):
    tm, width = q_ref.shape
    hd = width // MEM_HEADS
    q = q_ref[...]
    if tm < 8:
        q = jnp.broadcast_to(q[0:1], (8, width))
    for h in range(MEM_HEADS):
        cs = slice(h * hd, (h + 1) * hd)
        s = lax.dot_general(q[:, cs].astype(BF16), k_ref[:, cs].astype(BF16), NT_DIMS,
                            preferred_element_type=F32) * (hd ** -0.5)
        e = jnp.exp(s - jnp.max(s, axis=-1, keepdims=True))
        p = e * (1.0 / jnp.sum(e, axis=-1, keepdims=True))
        o = jnp.dot(p.astype(BF16), v_ref[:, cs].astype(BF16), preferred_element_type=F32)
        o_ref[:, cs] = (o[:tm] * jax.nn.silu(zm_ref[:, cs])).astype(o_ref.dtype)


def mem_attention(p, mem_k, mem_v, *, lead_k, lead_v, q_block, tm, name):
    B, L, _ = p.shape
    mlen, width = mem_k.shape[-2:]
    tm = min(tm, L)
    mem_spec = lambda lead: pl.BlockSpec((None,) * (len(lead) + 1) + (mlen, width),
                                         lambda b, i: (*lead, b, 0, 0))
    return pl.pallas_call(
        _mem_attn_kernel,
        grid=(B, L // tm),
        in_specs=[pl.BlockSpec((None, tm, width), lambda b, i: (b, i, q_block)),
                  pl.BlockSpec((None, tm, width), lambda b, i: (b, i, q_block + 1)),
                  mem_spec(lead_k), mem_spec(lead_v)],
        out_specs=pl.BlockSpec((None, tm, width), lambda b, i: (b, i, 0)),
        out_shape=jax.ShapeDtypeStruct((B, L, width), BF16),
        compiler_params=_params("parallel", "parallel"),
        name=name,
    )(p, p, mem_k, mem_v)


def _rel_bucket(dist):
    n = jnp.maximum(dist, 0)
    max_exact = N_BUCKETS // 2
    nf = jnp.maximum(n, 1).astype(F32)
    large = max_exact + (jnp.log(nf / max_exact) / math.log(MAX_DISTANCE / max_exact)
                         * (N_BUCKETS - max_exact)).astype(jnp.int32)
    large = jnp.minimum(large, N_BUCKETS - 1)
    return jnp.where(n < max_exact, n, large)


def _lambda(lam_ref, lam_init):
    a = jnp.sum(lam_ref[0:1, :] * lam_ref[1:2, :], axis=-1, keepdims=True)
    b = jnp.sum(lam_ref[2:3, :] * lam_ref[3:4, :], axis=-1, keepdims=True)
    return jnp.exp(a) - jnp.exp(b) + lam_init


def _subln_gate(o, g, z, lam_init):
    ms = jnp.mean(o * o, axis=-1, keepdims=True)
    on = (o * lax.rsqrt(ms + EPS) * g) * (1.0 - lam_init)
    return on * jax.nn.silu(z)


def _lane_groups(x):
    return [x[:, g * 128:(g + 1) * 128] for g in range(x.shape[1] // 128)]


def _diff_attn_kernel(q_ref, z_ref, k_ref, v_ref, bd_ref, be_ref, bfar_ref, lam_ref, g_ref, o_ref,
                      vb_ref, b0_ref, b1_ref, s_ref, mp_ref, lp_ref, acc_ref, *, lam_init):
    t = q_ref.shape[0]
    qi = pl.program_id(2)
    scale = B_HD ** -0.5
    md = MAX_DISTANCE
    nb = t // md

    @pl.when(qi == 0)
    def _():
        vb_ref[...] = v_ref[...].astype(BF16)
        far = jnp.broadcast_to(bfar_ref[0:1, 0:1], (md, md))
        masked = jnp.full((md, md), NEG, F32)
        near, next_to_near = bd_ref[...], be_ref[...]
        for bi in range(nb):
            for bj in range(nb):
                piece = near if bi == bj else next_to_near if bi == bj + 1 else far if bi > bj else masked
                b0_ref[bi * md:(bi + 1) * md, bj * md:(bj + 1) * md] = piece
                piece = next_to_near if (bi == 0 and bj == nb - 1) else far
                b1_ref[bi * md:(bi + 1) * md, bj * md:(bj + 1) * md] = piece

    q = q_ref[...]
    qs = (q[:, :B_HD].astype(BF16), q[:, B_HD:].astype(BF16))
    mp_ref[...] = jnp.full(mp_ref.shape, NEG, F32)
    lp_ref[...] = jnp.zeros(lp_ref.shape, F32)
    acc_ref[...] = jnp.zeros(acc_ref.shape, F32)

    def scores(kj, bias):
        off = pl.multiple_of(kj * t, t)
        k = k_ref[pl.ds(off, t), :]
        for c in range(2):
            s = lax.dot_general(qs[c], k[:, c * B_HD:(c + 1) * B_HD], NT_DIMS,
                                preferred_element_type=F32) * scale + bias
            s_ref[c, :, pl.ds(off, t)] = s
            mp_ref[c] = functools.reduce(jnp.maximum, _lane_groups(s), mp_ref[c])

    bfar = bfar_ref[0:1, 0:1]

    def far(kj, carry):
        scores(kj, bfar)
        return carry

    lax.fori_loop(0, jnp.maximum(qi - 1, 0), far, 0)

    @pl.when(qi >= 1)
    def _():
        scores(qi - 1, b1_ref[...])

    scores(qi, b0_ref[...])

    m_rows = [jnp.broadcast_to(jnp.max(mp_ref[c], axis=-1, keepdims=True), (t, 128)) for c in range(2)]

    def accumulate(kj, carry):
        off = pl.multiple_of(kj * t, t)
        v = vb_ref[pl.ds(off, t), :]
        for c in range(2):
            ps = [jnp.exp(sg - m_rows[c]) for sg in _lane_groups(s_ref[c, :, pl.ds(off, t)])]
            lp_ref[c] = functools.reduce(jnp.add, ps, lp_ref[c])
            p = jnp.concatenate(ps, axis=1).astype(BF16)
            acc_ref[c] = acc_ref[c] + jnp.dot(p, v, preferred_element_type=F32)
        return carry

    lax.fori_loop(0, qi + 1, accumulate, 0)

    lam = _lambda(lam_ref, lam_init)
    inv_l = [1.0 / jnp.sum(lp_ref[c], axis=-1, keepdims=True) for c in range(2)]
    o = acc_ref[0] * inv_l[0] - lam * (acc_ref[1] * inv_l[1])
    o_ref[...] = _subln_gate(o, g_ref[...], z_ref[...], lam_init).astype(o_ref.dtype)


def _toeplitz_tiles(g, t):
    H, P = g.shape
    hankel = jnp.tile(g[:, ::-1], (1, t + 1))[:, :t * (P + 1)].reshape(H, t, P + 1)
    return hankel[:, ::-1, :t]


def diff_attention_prompt(p, k, v, rel_bias, lam_rows, subln_g, *, lam_init, t, name):
    B, L, _ = p.shape
    H = k.shape[1]
    md = MAX_DISTANCE
    assert t % md == 0 and L % t == 0
    table = rel_bias.astype(F32)
    d = jnp.arange(-(md - 1), md, dtype=jnp.int32)
    bd = _toeplitz_tiles(jnp.where((d >= 0)[None], table[_rel_bucket(d)].T, NEG), md)
    be = _toeplitz_tiles(table[_rel_bucket(d + md)].T, md)
    bfar = jnp.broadcast_to(table[_rel_bucket(jnp.int32(md + 1))][:, None, None], (H, 1, 128))
    blk = lambda off: pl.BlockSpec((None, t, B_HD2), lambda b, h, i: (b, i, off + h))
    kv = pl.BlockSpec((None, None, L, B_HD2), lambda b, h, i: (b, h, 0, 0))
    tile = pl.BlockSpec((None, md, md), lambda b, h, i: (h, 0, 0))
    return pl.pallas_call(
        functools.partial(_diff_attn_kernel, lam_init=lam_init),
        grid=(B, H, L // t),
        in_specs=[blk(0), blk(H), kv, kv, tile, tile,
                  pl.BlockSpec((None, 1, 128), lambda b, h, i: (h, 0, 0)),
                  pl.BlockSpec((4, B_HD), lambda b, h, i: (0, 0)),
                  pl.BlockSpec((1, B_HD2), lambda b, h, i: (0, 0))],
        out_specs=pl.BlockSpec((None, t, B_HD2), lambda b, h, i: (b, i, h)),
        out_shape=jax.ShapeDtypeStruct((B, L, H * B_HD2), BF16),
        scratch_shapes=[pltpu.VMEM((L, B_HD2), BF16), pltpu.VMEM((t, t), F32), pltpu.VMEM((t, t), F32),
                        pltpu.VMEM((2, t, L), F32), pltpu.VMEM((2, t, 128), F32), pltpu.VMEM((2, t, 128), F32),
                        pltpu.VMEM((2, t, B_HD2), F32)],
        compiler_params=_params("parallel", "parallel", "arbitrary"),
        name=name,
    )(p, p, k, v, bd, be, bfar, lam_rows, subln_g.reshape(1, B_HD2))


def _in_proj_decode_kernel(pt_ref, x_ref, g_ref, w_ref, ps_ref, kn_ref, vn_ref, sb_ref, lam_ref, sg_ref, *rest,
                           pps, lam_init, steps_per_seq):
    del pt_ref
    k_pages, v_pages = rest[:pps], rest[pps:2 * pps]
    o_ref, d_ref, h_ref, qb_ref, m_ref, l_ref, acc_ref = rest[2 * pps:]
    H, rows, _ = qb_ref.shape
    width = H * B_HD2
    step = pl.program_id(0) * pl.num_programs(1) + pl.program_id(1)
    b = step // steps_per_seq
    first = step % steps_per_seq == 0
    last = step % steps_per_seq == steps_per_seq - 1
    scale = B_HD ** -0.5

    def query_rows(hd):
        r_id = lax.broadcasted_iota(jnp.int32, (rows, B_HD2), 0)
        c_id = lax.broadcasted_iota(jnp.int32, (rows, B_HD2), 1)
        own = (c_id >> 7) == r_id
        return jnp.where(own, ps_ref[pl.ds(b, 1), hd * B_HD2:(hd + 1) * B_HD2], 0.0)

    @pl.when(first)
    def _():
        for hd in range(H):
            qb_ref[hd] = query_rows(hd).astype(BF16)
        m_ref[...] = jnp.full(m_ref.shape, NEG, F32)
        l_ref[...] = jnp.zeros(l_ref.shape, F32)
        acc_ref[...] = jnp.zeros(acc_ref.shape, F32)

    def softmax_step(hd, s):
        m_old = m_ref[hd]
        m_new = jnp.maximum(m_old, jnp.max(s, axis=-1, keepdims=True))
        alpha = jnp.exp(m_old - m_new)
        p = jnp.exp(s - m_new)
        l_ref[hd] = alpha * l_ref[hd] + jnp.sum(p, axis=-1, keepdims=True)
        m_ref[hd] = m_new
        return alpha, p

    def head_scores(hd):
        kh = jnp.concatenate(
            [jnp.concatenate([kp[pl.ds(2 * hd + c, PAGE_SIZE, stride=2 * H), :] for c in range(2)], axis=1)
             for kp in k_pages], axis=0).astype(BF16)
        bias = jnp.concatenate([sb_ref[0, hd:hd + 1, :]] * (pps - 1)
                               + [jnp.where(last, sb_ref[1, hd:hd + 1, :], sb_ref[0, hd:hd + 1, :])], axis=1)
        return lax.dot_general(qb_ref[hd], kh, NT_DIMS, preferred_element_type=F32) * scale + bias

    def head_values(hd, alpha, p):
        vh = jnp.concatenate([vp[hd] for vp in v_pages], axis=0).astype(BF16)
        acc_ref[hd] = alpha * acc_ref[hd] + jnp.dot(p.astype(BF16), vh, preferred_element_type=F32)

    kc = min(MXU_DEPTH_V7X, x_ref.shape[1])
    n_chunks = x_ref.shape[1] // kc
    lag = 4

    def tile_and_pages(normalise):
        if normalise:
            x = x_ref[...]
            r = lax.rsqrt(jnp.mean(x * x, axis=-1, keepdims=True) + EPS)
        folded = {}
        acc = None
        for c in range(n_chunks):
            cs = slice(c * kc, (c + 1) * kc)
            if normalise:
                hc = (x_ref[:, cs] * r * g_ref[:, cs]).astype(BF16)
                h_ref[:, cs] = hc
            else:
                hc = h_ref[:, cs]
            part = jnp.dot(hc, w_ref[cs, :], preferred_element_type=F32)
            acc = part if acc is None else acc + part
            for hd in range(H):
                if hd * n_chunks // (H + lag) == c:
                    folded[hd] = softmax_step(hd, head_scores(hd))
            for hd in range(H):
                if (hd + lag) * n_chunks // (H + lag) == c:
                    head_values(hd, *folded[hd])
        o_ref[...] = acc

    first_col = pl.program_id(1) == 0
    pl.when(first_col)(lambda: tile_and_pages(True))
    pl.when(jnp.logical_not(first_col))(lambda: tile_and_pages(False))

    @pl.when(last)
    def _():
        lam = _lambda(lam_ref, lam_init)
        for hd in range(H):
            cs = slice(hd * B_HD2, (hd + 1) * B_HD2)
            s = (jnp.sum(query_rows(hd) * kn_ref[pl.ds(b, 1), cs], axis=-1, keepdims=True) * scale
                 + sb_ref[2, hd:hd + 1, 0:1])
            alpha, p = softmax_step(hd, s)
            o_both = (alpha * acc_ref[hd] + p * vn_ref[pl.ds(b, 1), cs]) * (1.0 / l_ref[hd])
            o = o_both[0:1] - lam * o_both[1:2]
            z = ps_ref[pl.ds(b, 1), width + hd * B_HD2:width + (hd + 1) * B_HD2]
            d_ref[pl.ds(b, 1), cs] = _subln_gate(o, sg_ref[...], z, lam_init)


def in_proj_with_decode(x, g, w, p_s, k_new, v_new, cache_k, cache_v, page_table, rel_bias, lam_rows, subln_g,
                        *, lam_init, tm, name):
    M, D = x.shape
    nj, _, tn = w.shape
    N = nj * tn
    B, n_pages = page_table.shape
    width = k_new.shape[-1]
    H = width // B_HD2
    k_rows = PAGE_SIZE * 2 * H
    ni = M // tm
    assert (B * n_pages) % (ni * nj) == 0, "every grid step takes the same number of pages"
    pps = B * n_pages // (ni * nj)
    assert n_pages % pps == 0 and PAGE_SIZE >= MAX_DISTANCE
    rows = 8
    table = rel_bias.astype(F32)
    dist_last = PAGE_SIZE - jnp.arange(PAGE_SIZE, dtype=jnp.int32)
    sb = jnp.stack([
        jnp.broadcast_to(table[_rel_bucket(jnp.int32(PAGE_SIZE + 1))][:, None], (H, PAGE_SIZE)),
        table[_rel_bucket(dist_last)].T,
        jnp.broadcast_to(table[_rel_bucket(jnp.int32(0))][:, None], (H, PAGE_SIZE)),
    ])
    sg = subln_g.reshape(1, B_HD2)
    whole = lambda a: pl.BlockSpec(a.shape, lambda i, j, pt: (0,) * a.ndim)
    k_page = lambda r: pl.BlockSpec((k_rows, B_HD), lambda i, j, pt: (pt[(i * nj + j) * pps + r], 0))
    v_page = lambda r: pl.BlockSpec((None, H, PAGE_SIZE, B_HD2),
                                    lambda i, j, pt: (pt[(i * nj + j) * pps + r], 0, 0, 0))
    grid_spec = pltpu.PrefetchScalarGridSpec(
        num_scalar_prefetch=1,
        grid=(ni, nj),
        in_specs=[pl.BlockSpec((tm, D), lambda i, j, pt: (i, 0)),
                  pl.BlockSpec((1, D), lambda i, j, pt: (0, 0)),
                  pl.BlockSpec((None, D, tn), lambda i, j, pt: (j, 0, 0)),
                  whole(p_s), whole(k_new), whole(v_new), whole(sb), whole(lam_rows), whole(sg)]
                 + [k_page(r) for r in range(pps)] + [v_page(r) for r in range(pps)],
        out_specs=[pl.BlockSpec((tm, tn), lambda i, j, pt: (i, j)),
                   pl.BlockSpec((B, width), lambda i, j, pt: (0, 0))],
        scratch_shapes=[pltpu.VMEM((tm, D), BF16), pltpu.VMEM((H, rows, B_HD2), BF16),
                        pltpu.VMEM((H, rows, 1), F32), pltpu.VMEM((H, rows, 1), F32),
                        pltpu.VMEM((H, rows, B_HD2), F32)],
    )
    return pl.pallas_call(
        functools.partial(_in_proj_decode_kernel, pps=pps, lam_init=lam_init, steps_per_seq=n_pages // pps),
        grid_spec=grid_spec,
        out_shape=[jax.ShapeDtypeStruct((M, N), F32), jax.ShapeDtypeStruct((B, width), F32)],
        compiler_params=_params("arbitrary", "arbitrary"),
        name=name,
    )(page_table.reshape(-1), x, g.reshape(1, D), w, p_s, k_new, v_new, sb, lam_rows, sg,
      *([cache_k] * pps), *([cache_v] * pps))


def _trunk(x_prompt, x_sample, mkv, smk, smv, past, W):
    bp, L, D = x_prompt.shape
    bs = x_sample.shape[0]
    mp = bp * L
    depth = W["norm_g"].shape[0]
    n_a = W["w_in_a"].shape[0]
    e_a = W["sgu_ln_g"].shape[1]
    b_width = W["w_kv"].shape[1] // 2
    mem_width = mkv.shape[-1]
    H = b_width // B_HD2
    tm, tn = 512, 1024
    tn_cast = 512
    xp = x_prompt.reshape(mp, D)
    xs = x_sample.reshape(bs, D)
    chunk_vs = []

    def mem_both(l, pp, ps, q_block):
        mo_p = mem_attention(pp.reshape(bp, L, -1), mkv, mkv, lead_k=(l, 0), lead_v=(l, 1),
                             q_block=q_block, tm=512, name=f"mem{l}_p")
        mo_s = mem_attention(ps.reshape(bs, 1, -1), smk, smv, lead_k=(l,), lead_v=(l,),
                             q_block=q_block, tm=1, name=f"mem{l}_s")
        return mo_p.reshape(mp, mem_width), mo_s.reshape(bs, mem_width)

    for l in range(depth):
        g_in = W["norm_g"][l]
        if l < n_a:
            ps, wb = rms_matmul_cast(xs, g_in, W["w_in_a"], layer=l, tn=tn_cast, name=f"in_a{l}_s")
            pp = rms_matmul(xp, g_in, wb, tm=tm, tn=tn, name=f"in_a{l}_p")
            sgu_w = (W["sgu_ln_g"][l], W["sgu_ln_b"][l], W["sgu_w"][l], W["sgu_b"][l])
            main_s, vn = sgu_first_row(ps, *sgu_w, width=e_a, name=f"sgu{l}_s")
            chunk_vs.append(vn)
            main_p = sgu_prompt(pp, *sgu_w, width=e_a, rows=2 * CHUNK, name=f"sgu{l}_p")
            q_block = 3 * e_a // mem_width
            w_out, w_layer = W["w_out_a"], l
        else:
            jb = l - n_a
            lam_init = 0.8 - 0.6 * math.exp(-0.3 * l)
            lam_rows = jnp.stack([W["lambda_q1"][jb], W["lambda_k1"][jb], W["lambda_q2"][jb], W["lambda_k2"][jb]])
            ps, wb = rms_matmul_cast(xs, g_in, W["w_in_b"], layer=jb, tn=tn_cast, column_blocked=True,
                                     name=f"in_b{jb}_s")
            pp, main_s = in_proj_with_decode(xp, g_in, wb, ps, k_s, v_s, *past, W["rel_bias"], lam_rows,
                                             W["subln_g"][jb], lam_init=lam_init, tm=tm,
                                             name=f"in_b{jb}_p_dattn{jb}_s")
            main_s = main_s.astype(BF16)
            main_p = diff_attention_prompt(pp.reshape(bp, L, -1), k_att, v_p, W["rel_bias"], lam_rows,
                                           W["subln_g"][jb], lam_init=lam_init, t=512, name=f"dattn{jb}_p")
            main_p = main_p.reshape(mp, b_width)
            q_block = 2 * b_width // mem_width
            w_out, w_layer = W["w_out_b"], jb
        mo_p, mo_s = mem_both(l, pp, ps, q_block)
        xs, wb = out_proj_cast(main_s, mo_s, w_out, xs, layer=w_layer, tn=tn_cast, name=f"out{l}_s")
        if l < depth - 1:
            xp = out_proj(main_p, mo_p, wb, xp, tm=tm, tn=tn, name=f"out{l}_p")
        else:
            y_p = out_proj_norm(main_p, mo_p, wb, xp, W["final_norm_g"], tm=tm, tn=tn, name=f"out{l}_p")
        if l == n_a - 1:
            g_kv = W["kv_norm_g"]
            k_s, wb_k = rms_matmul_cast(xs, g_kv, W["w_kv"], col0=0, n_cols=b_width, tn=tn_cast, name="k_s")
            v_s, wb_v = rms_matmul_cast(xs, g_kv, W["w_kv"], col0=b_width, n_cols=b_width, tn=tn_cast, name="v_s")
            k_p, k_att = k_proj_prompt(xp, g_kv, wb_k, B=bp, tm=tm, heads=2, name="k_p")
            v_p = v_proj_prompt(xp, g_kv, wb_v, B=bp, tm=tm, heads=4, name="v_p")
    y_s = rmsnorm_rows(xs, W["final_norm_g"], tm=bs, name="final_s")
    return (y_p.reshape(bp, L, D), y_s.reshape(bs, 1, D),
            k_p.reshape(bp, L, H, 2, B_HD), jnp.transpose(v_p, (0, 2, 1, 3)),
            k_s.reshape(bs, 1, H, 2, B_HD), v_s.reshape(bs, 1, H, B_HD2), chunk_vs)


def kernel(x_prompt, x_sample, cache_k, cache_v, cache_mem_k, cache_mem_v, page_table, mem_prompt,
           norm_g, final_norm_g, mem_norm_g, w_mem_kv, w_in_a, sgu_ln_g, sgu_ln_b, sgu_w, sgu_b,
           w_out_a, kv_norm_g, w_kv, w_in_b, lambda_q1, lambda_k1, lambda_q2, lambda_k2,
           subln_g, rel_bias, w_out_b):
    depth = norm_g.shape[0]
    b_width = w_kv.shape[1] // 2
    mem_width = w_mem_kv.shape[-1] // 2
    mem_hd = mem_width // MEM_HEADS
    W = dict(norm_g=norm_g, final_norm_g=final_norm_g, sgu_ln_g=sgu_ln_g, sgu_ln_b=sgu_ln_b, sgu_w=sgu_w,
             sgu_b=sgu_b, kv_norm_g=kv_norm_g, lambda_q1=lambda_q1, lambda_k1=lambda_k1, lambda_q2=lambda_q2,
             lambda_k2=lambda_k2, subln_g=subln_g, rel_bias=rel_bias,
             w_in_a=w_in_a, w_out_a=w_out_a, w_in_b=w_in_b, w_out_b=w_out_b, w_kv=w_kv)

    bd, dec_seq = x_sample.shape[:2]
    assert dec_seq == 1, "the sample group is one new token per sequence"
    past = (cache_k.reshape(-1, B_HD), jnp.transpose(cache_v, (0, 2, 1, 3)), page_table)
    smk = cache_mem_k.reshape(depth, bd, -1, mem_width)
    smv = cache_mem_v.reshape(depth, bd, -1, mem_width)

    bp, m_len, d_model = mem_prompt.shape
    mkv = rms_matmul_stacked(mem_prompt.reshape(bp * m_len, d_model), mem_norm_g, w_mem_kv.astype(BF16),
                             tm=512, tn=mem_width, name="mem_kv")
    mkv = mkv.reshape(depth, 2, bp, m_len, mem_width)

    y_p, y_s, k_p, v_p, k_s, v_s, chunk_vs = _trunk(x_prompt, x_sample, mkv, smk, smv, past, W)
    chunk_v_s = jnp.stack(chunk_vs).reshape(len(chunk_vs), bd, dec_seq, -1)
    mem_shape = (depth, bp, m_len, MEM_HEADS, mem_hd)
    return (y_p, y_s, k_p, v_p, k_s, v_s, mkv[:, 0].reshape(mem_shape), mkv[:, 1].reshape(mem_shape), chunk_v_s)
```

```python
import functools
import math

import jax
import jax.numpy as jnp
from jax import lax
from jax.experimental import pallas as pl
from jax.experimental.pallas import tpu as pltpu

F32 = jnp.float32
BF16 = jnp.bfloat16

CHUNK = 128
SGU_GROUPS = 8
B_HD2 = 256
B_HD = B_HD2 // 2
MEM_HEADS = 4
N_BUCKETS = 32
MAX_DISTANCE = 128
PAGE_SIZE = 128
EPS = 1e-6
NEG = -1e30

VMEM_BYTES_V7X = 64 * 1024 * 1024
MXU_DEPTH_V7X = 256
VMEM_LIMIT = VMEM_BYTES_V7X - 8 * 1024 * 1024

NT_DIMS = (((1,), (1,)), ((), ()))


def _params(*sem):
    return pltpu.CompilerParams(dimension_semantics=sem, vmem_limit_bytes=VMEM_LIMIT)


def _normalised_rows(x_ref, g_ref, h_ref):
    @pl.when(pl.program_id(1) == 0)
    def _():
        x = x_ref[...]
        ms = jnp.mean(x * x, axis=-1, keepdims=True)
        h_ref[...] = (x * lax.rsqrt(ms + EPS) * g_ref[...]).astype(BF16)

    return h_ref[...]


def _rms_dot(x_ref, g_ref, w_ref, h_ref, emit):
    first = pl.program_id(1) == 0

    @pl.when(first)
    def _():
        x = x_ref[...]
        r = lax.rsqrt(jnp.mean(x * x, axis=-1, keepdims=True) + EPS)
        kc = min(MXU_DEPTH_V7X, x_ref.shape[1])
        acc = None
        for c in range(x_ref.shape[1] // kc):
            cs = slice(c * kc, (c + 1) * kc)
            hc = (x_ref[:, cs] * r * g_ref[:, cs]).astype(BF16)
            h_ref[:, cs] = hc
            part = jnp.dot(hc, w_ref[cs, :], preferred_element_type=F32)
            acc = part if acc is None else acc + part
        emit(acc)

    @pl.when(jnp.logical_not(first))
    def _():
        emit(jnp.dot(h_ref[...], w_ref[...], preferred_element_type=F32))


def _rms_matmul_kernel(x_ref, g_ref, w_ref, o_ref, h_ref):
    def emit(res):
        o_ref[...] = res

    _rms_dot(x_ref, g_ref, w_ref, h_ref, emit)


def rms_matmul(x, g, w, *, tm, tn, name):
    M, D = x.shape
    N = w.shape[1]
    return pl.pallas_call(
        _rms_matmul_kernel,
        grid=(M // tm, N // tn),
        in_specs=[pl.BlockSpec((tm, D), lambda i, j: (i, 0)),
                  pl.BlockSpec((1, D), lambda i, j: (0, 0)),
                  pl.BlockSpec((D, tn), lambda i, j: (0, j))],
        out_specs=pl.BlockSpec((tm, tn), lambda i, j: (i, j)),
        out_shape=jax.ShapeDtypeStruct((M, N), F32),
        scratch_shapes=[pltpu.VMEM((tm, D), BF16)],
        compiler_params=_params("parallel", "arbitrary"),
        name=name,
    )(x, g.reshape(1, D), w)


def _rms_matmul_cast_kernel(x_ref, g_ref, w_ref, o_ref, wb_ref, h_ref):
    h = _normalised_rows(x_ref, g_ref, h_ref)
    wb = w_ref[...].astype(BF16)
    wb_ref[...] = wb
    o_ref[...] = jnp.dot(h, wb, preferred_element_type=F32)


def _f32_weight_spec(w, layer, tn, first_block=0):
    if w.ndim == 2:
        return pl.BlockSpec((w.shape[0], tn), lambda i, j: (0, first_block + j))
    return pl.BlockSpec((None, w.shape[1], tn), lambda i, j: (layer, 0, first_block + j))


def rms_matmul_cast(x, g, w, *, tn, name, layer=0, col0=0, n_cols=None, column_blocked=False):
    M, D = x.shape
    n_cols = w.shape[-1] if n_cols is None else n_cols
    if column_blocked:
        wb_spec = pl.BlockSpec((None, D, tn), lambda i, j: (j, 0, 0))
        wb_shape = jax.ShapeDtypeStruct((n_cols // tn, D, tn), BF16)
    else:
        wb_spec = pl.BlockSpec((D, tn), lambda i, j: (0, j))
        wb_shape = jax.ShapeDtypeStruct((D, n_cols), BF16)
    return pl.pallas_call(
        _rms_matmul_cast_kernel,
        grid=(1, n_cols // tn),
        in_specs=[pl.BlockSpec((M, D), lambda i, j: (0, 0)),
                  pl.BlockSpec((1, D), lambda i, j: (0, 0)),
                  _f32_weight_spec(w, layer, tn, col0 // tn)],
        out_specs=[pl.BlockSpec((M, tn), lambda i, j: (0, j)), wb_spec],
        out_shape=[jax.ShapeDtypeStruct((M, n_cols), F32), wb_shape],
        scratch_shapes=[pltpu.VMEM((M, D), BF16)],
        compiler_params=_params("arbitrary", "arbitrary"),
        name=name,
    )(x, g.reshape(1, D), w)


def rms_matmul_stacked(x, g, w, *, tm, tn, name):
    M, D = x.shape
    G, _, N = w.shape
    nb = N // tn
    return pl.pallas_call(
        _rms_matmul_kernel,
        grid=(M // tm, G * nb),
        in_specs=[pl.BlockSpec((tm, D), lambda i, j: (i, 0)),
                  pl.BlockSpec((1, D), lambda i, j: (0, 0)),
                  pl.BlockSpec((None, D, tn), lambda i, j: (j // nb, 0, j % nb))],
        out_specs=pl.BlockSpec((None, tm, tn), lambda i, j: (j, i, 0)),
        out_shape=jax.ShapeDtypeStruct((G * nb, M, tn), F32),
        scratch_shapes=[pltpu.VMEM((tm, D), BF16)],
        compiler_params=_params("parallel", "arbitrary"),
        name=name,
    )(x, g.reshape(1, D), w)


def _k_proj_kernel(x_ref, g_ref, w_ref, knat_ref, katt_ref, h_ref):
    tm = x_ref.shape[0]
    heads = katt_ref.shape[0]
    j = pl.program_id(1)
    n_sub = knat_ref.shape[0] // tm

    def emit(res):
        for q in range(2 * heads):
            knat_ref[pl.ds(j * 2 * heads + q, tm, stride=n_sub), :] = res[:, q * B_HD:(q + 1) * B_HD]
        for hh in range(heads):
            katt_ref[hh] = res[:, hh * B_HD2:(hh + 1) * B_HD2].astype(BF16)

    _rms_dot(x_ref, g_ref, w_ref, h_ref, emit)


def k_proj_prompt(x, g, w, *, B, tm, heads, name):
    M, D = x.shape
    L = M // B
    N = w.shape[1]
    H = N // B_HD2
    nlb = L // tm
    return pl.pallas_call(
        _k_proj_kernel,
        grid=(M // tm, H // heads),
        in_specs=[pl.BlockSpec((tm, D), lambda i, j: (i, 0)),
                  pl.BlockSpec((1, D), lambda i, j: (0, 0)),
                  pl.BlockSpec((D, heads * B_HD2), lambda i, j: (0, j))],
        out_specs=[pl.BlockSpec((tm * 2 * H, B_HD), lambda i, j: (i, 0)),
                   pl.BlockSpec((None, heads, tm, B_HD2), lambda i, j: (i // nlb, j, i % nlb, 0))],
        out_shape=[jax.ShapeDtypeStruct((M * 2 * H, B_HD), F32),
                   jax.ShapeDtypeStruct((B, H, L, B_HD2), BF16)],
        scratch_shapes=[pltpu.VMEM((tm, D), BF16)],
        compiler_params=_params("parallel", "arbitrary"),
        name=name,
    )(x, g.reshape(1, D), w)


def _v_proj_kernel(x_ref, g_ref, w_ref, v_ref, h_ref):
    def emit(res):
        for hh in range(v_ref.shape[0]):
            v_ref[hh] = res[:, hh * B_HD2:(hh + 1) * B_HD2]

    _rms_dot(x_ref, g_ref, w_ref, h_ref, emit)


def v_proj_prompt(x, g, w, *, B, tm, heads, name):
    M, D = x.shape
    L = M // B
    H = w.shape[1] // B_HD2
    nlb = L // tm
    return pl.pallas_call(
        _v_proj_kernel,
        grid=(M // tm, H // heads),
        in_specs=[pl.BlockSpec((tm, D), lambda i, j: (i, 0)),
                  pl.BlockSpec((1, D), lambda i, j: (0, 0)),
                  pl.BlockSpec((D, heads * B_HD2), lambda i, j: (0, j))],
        out_specs=pl.BlockSpec((None, heads, tm, B_HD2), lambda i, j: (i // nlb, j, i % nlb, 0)),
        out_shape=jax.ShapeDtypeStruct((B, H, L, B_HD2), F32),
        scratch_shapes=[pltpu.VMEM((tm, D), BF16)],
        compiler_params=_params("parallel", "arbitrary"),
        name=name,
    )(x, g.reshape(1, D), w)


def _out_proj_kernel(a_ref, m_ref, w_ref, x_ref, o_ref):
    ka = a_ref.shape[1]
    acc = jnp.dot(a_ref[...], w_ref[:ka, :], preferred_element_type=F32)
    acc = acc + jnp.dot(m_ref[...], w_ref[ka:, :], preferred_element_type=F32)
    o_ref[...] = x_ref[...] + acc


def out_proj(a, m, w, x, *, tm, tn, name):
    M, ka = a.shape
    km = m.shape[1]
    N = w.shape[1]
    return pl.pallas_call(
        _out_proj_kernel,
        grid=(M // tm, N // tn),
        in_specs=[pl.BlockSpec((tm, ka), lambda i, j: (i, 0)),
                  pl.BlockSpec((tm, km), lambda i, j: (i, 0)),
                  pl.BlockSpec((ka + km, tn), lambda i, j: (0, j)),
                  pl.BlockSpec((tm, tn), lambda i, j: (i, j))],
        out_specs=pl.BlockSpec((tm, tn), lambda i, j: (i, j)),
        out_shape=jax.ShapeDtypeStruct((M, N), F32),
        compiler_params=_params("parallel", "arbitrary"),
        name=name,
    )(a, m, w, x)


def _out_proj_cast_kernel(a_ref, m_ref, w_ref, x_ref, o_ref, wb_ref):
    ka = a_ref.shape[1]
    wb = w_ref[...].astype(BF16)
    wb_ref[...] = wb
    acc = jnp.dot(a_ref[...], wb[:ka], preferred_element_type=F32)
    acc = acc + jnp.dot(m_ref[...], wb[ka:], preferred_element_type=F32)
    o_ref[...] = x_ref[...] + acc


def out_proj_cast(a, m, w, x, *, tn, name, layer=0):
    M, ka = a.shape
    km = m.shape[1]
    N = w.shape[-1]
    return pl.pallas_call(
        _out_proj_cast_kernel,
        grid=(1, N // tn),
        in_specs=[pl.BlockSpec((M, ka), lambda i, j: (0, 0)),
                  pl.BlockSpec((M, km), lambda i, j: (0, 0)),
                  _f32_weight_spec(w, layer, tn),
                  pl.BlockSpec((M, tn), lambda i, j: (0, j))],
        out_specs=[pl.BlockSpec((M, tn), lambda i, j: (0, j)),
                   pl.BlockSpec((ka + km, tn), lambda i, j: (0, j))],
        out_shape=[jax.ShapeDtypeStruct((M, N), F32),
                   jax.ShapeDtypeStruct((ka + km, N), BF16)],
        compiler_params=_params("arbitrary", "arbitrary"),
        name=name,
    )(a, m, w, x)


def _out_proj_norm_kernel(a_ref, m_ref, w_ref, x_ref, g_ref, o_ref):
    j = pl.program_id(1)
    ka = a_ref.shape[1]
    tn = x_ref.shape[1]
    acc = jnp.dot(a_ref[...], w_ref[:ka, :], preferred_element_type=F32)
    acc = acc + jnp.dot(m_ref[...], w_ref[ka:, :], preferred_element_type=F32)
    o_ref[:, pl.ds(pl.multiple_of(j * tn, tn), tn)] = x_ref[...] + acc

    @pl.when(j == pl.num_programs(1) - 1)
    def _():
        y = o_ref[...]
        ms = jnp.mean(y * y, axis=-1, keepdims=True)
        o_ref[...] = y * lax.rsqrt(ms + EPS) * g_ref[...]


def out_proj_norm(a, m, w, x, g, *, tm, tn, name):
    M, ka = a.shape
    km = m.shape[1]
    N = w.shape[-1]
    return pl.pallas_call(
        _out_proj_norm_kernel,
        grid=(M // tm, N // tn),
        in_specs=[pl.BlockSpec((tm, ka), lambda i, j: (i, 0)),
                  pl.BlockSpec((tm, km), lambda i, j: (i, 0)),
                  pl.BlockSpec((ka + km, tn), lambda i, j: (0, j)),
                  pl.BlockSpec((tm, tn), lambda i, j: (i, j)),
                  pl.BlockSpec((1, N), lambda i, j: (0, 0))],
        out_specs=pl.BlockSpec((tm, N), lambda i, j: (i, 0)),
        out_shape=jax.ShapeDtypeStruct((M, N), F32),
        compiler_params=_params("parallel", "arbitrary"),
        name=name,
    )(a, m, w, x, g.reshape(1, N))


def _rmsnorm_kernel(x_ref, g_ref, o_ref):
    x = x_ref[...]
    ms = jnp.mean(x * x, axis=-1, keepdims=True)
    o_ref[...] = x * lax.rsqrt(ms + EPS) * g_ref[...]


def rmsnorm_rows(x, g, *, tm, name):
    M, D = x.shape
    tm = min(tm, M)
    return pl.pallas_call(
        _rmsnorm_kernel,
        grid=(M // tm,),
        in_specs=[pl.BlockSpec((tm, D), lambda i: (i, 0)),
                  pl.BlockSpec((1, D), lambda i: (0, 0))],
        out_specs=pl.BlockSpec((tm, D), lambda i: (i, 0)),
        out_shape=jax.ShapeDtypeStruct((M, D), F32),
        compiler_params=_params("parallel"),
        name=name,
    )(x, g.reshape(1, D))


def _layernorm(v, g, b):
    mu = jnp.mean(v, axis=-1, keepdims=True)
    vc = v - mu
    var = jnp.mean(vc * vc, axis=-1, keepdims=True)
    return vc * lax.rsqrt(var + EPS) * g + b


def _sgu_kernel(u_ref, v_ref, z_ref, lng_ref, lnb_ref, w_ref, bt_ref, o_ref):
    rows, width = u_ref.shape
    ch = width // SGU_GROUPS
    r = lax.broadcasted_iota(jnp.int32, (CHUNK, CHUNK), 0)
    c = lax.broadcasted_iota(jnp.int32, (CHUNK, CHUNK), 1)
    causal = c <= r
    ws = [jnp.where(causal, w_ref[g], 0.0).astype(BF16) for g in range(SGU_GROUPS)]
    for n in range(rows // CHUNK):
        rs = slice(n * CHUNK, (n + 1) * CHUNK)
        vn = _layernorm(jax.nn.gelu(v_ref[rs, :]), lng_ref[...], lnb_ref[...]).astype(BF16)
        for g in range(SGU_GROUPS):
            cs = slice(g * ch, (g + 1) * ch)
            mixed = jnp.dot(ws[g], vn[:, cs], preferred_element_type=F32) + bt_ref[:, g:g + 1]
            u = jax.nn.gelu(u_ref[rs, cs])
            o_ref[rs, cs] = (u * mixed * jax.nn.silu(z_ref[rs, cs])).astype(o_ref.dtype)


def sgu_prompt(p, ln_g, ln_b, w_s, b_s, *, width, rows, name):
    M = p.shape[0]
    return pl.pallas_call(
        _sgu_kernel,
        grid=(M // rows,),
        in_specs=[pl.BlockSpec((rows, width), lambda i: (i, 0)),
                  pl.BlockSpec((rows, width), lambda i: (i, 1)),
                  pl.BlockSpec((rows, width), lambda i: (i, 2)),
                  pl.BlockSpec((1, width), lambda i: (0, 0)),
                  pl.BlockSpec((1, width), lambda i: (0, 0)),
                  pl.BlockSpec((SGU_GROUPS, CHUNK, CHUNK), lambda i: (0, 0, 0)),
                  pl.BlockSpec((CHUNK, SGU_GROUPS), lambda i: (0, 0))],
        out_specs=pl.BlockSpec((rows, width), lambda i: (i, 0)),
        out_shape=jax.ShapeDtypeStruct((M, width), BF16),
        compiler_params=_params("parallel"),
        name=name,
    )(p, p, p, ln_g.reshape(1, width), ln_b.reshape(1, width), w_s, b_s.T)


def _sgu_first_row_kernel(u_ref, v_ref, z_ref, lng_ref, lnb_ref, w0_ref, b0_ref, o_ref, vn_ref):
    vn = _layernorm(jax.nn.gelu(v_ref[...]), lng_ref[...], lnb_ref[...])
    vn_ref[...] = vn
    mixed = w0_ref[...] * vn + b0_ref[...]
    o_ref[...] = (jax.nn.gelu(u_ref[...]) * mixed * jax.nn.silu(z_ref[...])).astype(o_ref.dtype)


def sgu_first_row(p, ln_g, ln_b, w_s, b_s, *, width, name):
    M = p.shape[0]
    ch = width // SGU_GROUPS
    w0 = jnp.repeat(w_s[:, 0, 0], ch).reshape(1, width)
    b0 = jnp.repeat(b_s[:, 0], ch).reshape(1, width)
    row = pl.BlockSpec((1, width), lambda i: (0, 0))
    return pl.pallas_call(
        _sgu_first_row_kernel,
        grid=(1,),
        in_specs=[pl.BlockSpec((M, width), lambda i: (0, 0)),
                  pl.BlockSpec((M, width), lambda i: (0, 1)),
                  pl.BlockSpec((M, width), lambda i: (0, 2)),
                  row, row, row, row],
        out_specs=[pl.BlockSpec((M, width), lambda i: (0, 0)),
                   pl.BlockSpec((M, width), lambda i: (0, 0))],
        out_shape=[jax.ShapeDtypeStruct((M, width), BF16),
                   jax.ShapeDtypeStruct((M, width), F32)],
        compiler_params=_params("arbitrary"),
        name=name,
    )(p, p, p, ln_g.reshape(1, width), ln_b.reshape(1, width), w0, b0)


def _mem_attn_kernel(q_ref, zm_ref, k_ref, v_ref, o_ref):
    tm, width = q_ref.shape
    hd = width // MEM_HEADS
    q = q_ref[...]
    if tm < 8:
        q = jnp.broadcast_to(q[0:1], (8, width))
    for h in range(MEM_HEADS):
        cs = slice(h * hd, (h + 1) * hd)
        s = lax.dot_general(q[:, cs].astype(BF16), k_ref[:, cs].astype(BF16), NT_DIMS,
                            preferred_element_type=F32) * (hd ** -0.5)
        e = jnp.exp(s - jnp.max(s, axis=-1, keepdims=True))
        p = e * (1.0 / jnp.sum(e, axis=-1, keepdims=True))
        o = jnp.dot(p.astype(BF16), v_ref[:, cs].astype(BF16), preferred_element_type=F32)
        o_ref[:, cs] = (o[:tm] * jax.nn.silu(zm_ref[:, cs])).astype(o_ref.dtype)


def mem_attention(p, mem_k, mem_v, *, lead_k, lead_v, q_block, tm, name):
    B, L, _ = p.shape
    mlen, width = mem_k.shape[-2:]
    tm = min(tm, L)
    mem_spec = lambda lead: pl.BlockSpec((None,) * (len(lead) + 1) + (mlen, width),
                                         lambda b, i: (*lead, b, 0, 0))
    return pl.pallas_call(
        _mem_attn_kernel,
        grid=(B, L // tm),
        in_specs=[pl.BlockSpec((None, tm, width), lambda b, i: (b, i, q_block)),
                  pl.BlockSpec((None, tm, width), lambda b, i: (b, i, q_block + 1)),
                  mem_spec(lead_k), mem_spec(lead_v)],
        out_specs=pl.BlockSpec((None, tm, width), lambda b, i: (b, i, 0)),
        out_shape=jax.ShapeDtypeStruct((B, L, width), BF16),
        compiler_params=_params("parallel", "parallel"),
        name=name,
    )(p, p, mem_k, mem_v)


def _rel_bucket(dist):
    n = jnp.maximum(dist, 0)
    max_exact = N_BUCKETS // 2
    nf = jnp.maximum(n, 1).astype(F32)
    large = max_exact + (jnp.log(nf / max_exact) / math.log(MAX_DISTANCE / max_exact)
                         * (N_BUCKETS - max_exact)).astype(jnp.int32)
    large = jnp.minimum(large, N_BUCKETS - 1)
    return jnp.where(n < max_exact, n, large)


def _lambda(lam_ref, lam_init):
    a = jnp.sum(lam_ref[0:1, :] * lam_ref[1:2, :], axis=-1, keepdims=True)
    b = jnp.sum(lam_ref[2:3, :] * lam_ref[3:4, :], axis=-1, keepdims=True)
    return jnp.exp(a) - jnp.exp(b) + lam_init


def _subln_gate(o, g, z, lam_init):
    ms = jnp.mean(o * o, axis=-1, keepdims=True)
    on = (o * lax.rsqrt(ms + EPS) * g) * (1.0 - lam_init)
    return on * jax.nn.silu(z)


def _lane_groups(x):
    return [x[:, g * 128:(g + 1) * 128] for g in range(x.shape[1] // 128)]


def _diff_attn_kernel(q_ref, z_ref, k_ref, v_ref, bd_ref, be_ref, bfar_ref, lam_ref, g_ref, o_ref,
                      vb_ref, b0_ref, b1_ref, s_ref, mp_ref, lp_ref, acc_ref, *, lam_init):
    t = q_ref.shape[0]
    qi = pl.program_id(2)
    scale = B_HD ** -0.5
    md = MAX_DISTANCE
    nb = t // md

    @pl.when(qi == 0)
    def _():
        vb_ref[...] = v_ref[...].astype(BF16)
        far = jnp.broadcast_to(bfar_ref[0:1, 0:1], (md, md))
        masked = jnp.full((md, md), NEG, F32)
        near, next_to_near = bd_ref[...], be_ref[...]
        for bi in range(nb):
            for bj in range(nb):
                piece = near if bi == bj else next_to_near if bi == bj + 1 else far if bi > bj else masked
                b0_ref[bi * md:(bi + 1) * md, bj * md:(bj + 1) * md] = piece
                piece = next_to_near if (bi == 0 and bj == nb - 1) else far
                b1_ref[bi * md:(bi + 1) * md, bj * md:(bj + 1) * md] = piece

    q = q_ref[...]
    qs = (q[:, :B_HD].astype(BF16), q[:, B_HD:].astype(BF16))
    mp_ref[...] = jnp.full(mp_ref.shape, NEG, F32)
    lp_ref[...] = jnp.zeros(lp_ref.shape, F32)
    acc_ref[...] = jnp.zeros(acc_ref.shape, F32)

    def scores(kj, bias):
        off = pl.multiple_of(kj * t, t)
        k = k_ref[pl.ds(off, t), :]
        for c in range(2):
            s = lax.dot_general(qs[c], k[:, c * B_HD:(c + 1) * B_HD], NT_DIMS,
                                preferred_element_type=F32) * scale + bias
            s_ref[c, :, pl.ds(off, t)] = s
            mp_ref[c] = functools.reduce(jnp.maximum, _lane_groups(s), mp_ref[c])

    bfar = bfar_ref[0:1, 0:1]

    def far(kj, carry):
        scores(kj, bfar)
        return carry

    lax.fori_loop(0, jnp.maximum(qi - 1, 0), far, 0)

    @pl.when(qi >= 1)
    def _():
        scores(qi - 1, b1_ref[...])

    scores(qi, b0_ref[...])

    m_rows = [jnp.broadcast_to(jnp.max(mp_ref[c], axis=-1, keepdims=True), (t, 128)) for c in range(2)]

    def accumulate(kj, carry):
        off = pl.multiple_of(kj * t, t)
        v = vb_ref[pl.ds(off, t), :]
        for c in range(2):
            ps = [jnp.exp(sg - m_rows[c]) for sg in _lane_groups(s_ref[c, :, pl.ds(off, t)])]
            lp_ref[c] = functools.reduce(jnp.add, ps, lp_ref[c])
            p = jnp.concatenate(ps, axis=1).astype(BF16)
            acc_ref[c] = acc_ref[c] + jnp.dot(p, v, preferred_element_type=F32)
        return carry

    lax.fori_loop(0, qi + 1, accumulate, 0)

    lam = _lambda(lam_ref, lam_init)
    inv_l = [1.0 / jnp.sum(lp_ref[c], axis=-1, keepdims=True) for c in range(2)]
    o = acc_ref[0] * inv_l[0] - lam * (acc_ref[1] * inv_l[1])
    o_ref[...] = _subln_gate(o, g_ref[...], z_ref[...], lam_init).astype(o_ref.dtype)


def _toeplitz_tiles(g, t):
    H, P = g.shape
    hankel = jnp.tile(g[:, ::-1], (1, t + 1))[:, :t * (P + 1)].reshape(H, t, P + 1)
    return hankel[:, ::-1, :t]


def diff_attention_prompt(p, k, v, rel_bias, lam_rows, subln_g, *, lam_init, t, name):
    B, L, _ = p.shape
    H = k.shape[1]
    md = MAX_DISTANCE
    assert t % md == 0 and L % t == 0
    table = rel_bias.astype(F32)
    d = jnp.arange(-(md - 1), md, dtype=jnp.int32)
    bd = _toeplitz_tiles(jnp.where((d >= 0)[None], table[_rel_bucket(d)].T, NEG), md)
    be = _toeplitz_tiles(table[_rel_bucket(d + md)].T, md)
    bfar = jnp.broadcast_to(table[_rel_bucket(jnp.int32(md + 1))][:, None, None], (H, 1, 128))
    blk = lambda off: pl.BlockSpec((None, t, B_HD2), lambda b, h, i: (b, i, off + h))
    kv = pl.BlockSpec((None, None, L, B_HD2), lambda b, h, i: (b, h, 0, 0))
    tile = pl.BlockSpec((None, md, md), lambda b, h, i: (h, 0, 0))
    return pl.pallas_call(
        functools.partial(_diff_attn_kernel, lam_init=lam_init),
        grid=(B, H, L // t),
        in_specs=[blk(0), blk(H), kv, kv, tile, tile,
                  pl.BlockSpec((None, 1, 128), lambda b, h, i: (h, 0, 0)),
                  pl.BlockSpec((4, B_HD), lambda b, h, i: (0, 0)),
                  pl.BlockSpec((1, B_HD2), lambda b, h, i: (0, 0))],
        out_specs=pl.BlockSpec((None, t, B_HD2), lambda b, h, i: (b, i, h)),
        out_shape=jax.ShapeDtypeStruct((B, L, H * B_HD2), BF16),
        scratch_shapes=[pltpu.VMEM((L, B_HD2), BF16), pltpu.VMEM((t, t), F32), pltpu.VMEM((t, t), F32),
                        pltpu.VMEM((2, t, L), F32), pltpu.VMEM((2, t, 128), F32), pltpu.VMEM((2, t, 128), F32),
                        pltpu.VMEM((2, t, B_HD2), F32)],
        compiler_params=_params("parallel", "parallel", "arbitrary"),
        name=name,
    )(p, p, k, v, bd, be, bfar, lam_rows, subln_g.reshape(1, B_HD2))


def _in_proj_decode_kernel(pt_ref, x_ref, g_ref, w_ref, ps_ref, kn_ref, vn_ref, sb_ref, lam_ref, sg_ref, *rest,
                           pps, lam_init, steps_per_seq):
    del pt_ref
    k_pages, v_pages = rest[:pps], rest[pps:2 * pps]
    o_ref, d_ref, h_ref, qb_ref, m_ref, l_ref, acc_ref = rest[2 * pps:]
    H, rows, _ = qb_ref.shape
    width = H * B_HD2
    step = pl.program_id(0) * pl.num_programs(1) + pl.program_id(1)
    b = step // steps_per_seq
    first = step % steps_per_seq == 0
    last = step % steps_per_seq == steps_per_seq - 1
    scale = B_HD ** -0.5

    def query_rows(hd):
        r_id = lax.broadcasted_iota(jnp.int32, (rows, B_HD2), 0)
        c_id = lax.broadcasted_iota(jnp.int32, (rows, B_HD2), 1)
        own = (c_id >> 7) == r_id
        return jnp.where(own, ps_ref[pl.ds(b, 1), hd * B_HD2:(hd + 1) * B_HD2], 0.0)

    @pl.when(first)
    def _():
        for hd in range(H):
            qb_ref[hd] = query_rows(hd).astype(BF16)
        m_ref[...] = jnp.full(m_ref.shape, NEG, F32)
        l_ref[...] = jnp.zeros(l_ref.shape, F32)
        acc_ref[...] = jnp.zeros(acc_ref.shape, F32)

    def softmax_step(hd, s):
        m_old = m_ref[hd]
        m_new = jnp.maximum(m_old, jnp.max(s, axis=-1, keepdims=True))
        alpha = jnp.exp(m_old - m_new)
        p = jnp.exp(s - m_new)
        l_ref[hd] = alpha * l_ref[hd] + jnp.sum(p, axis=-1, keepdims=True)
        m_ref[hd] = m_new
        return alpha, p

    def head_scores(hd):
        kh = jnp.concatenate(
            [jnp.concatenate([kp[pl.ds(2 * hd + c, PAGE_SIZE, stride=2 * H), :] for c in range(2)], axis=1)
             for kp in k_pages], axis=0).astype(BF16)
        bias = jnp.concatenate([sb_ref[0, hd:hd + 1, :]] * (pps - 1)
                               + [jnp.where(last, sb_ref[1, hd:hd + 1, :], sb_ref[0, hd:hd + 1, :])], axis=1)
        return lax.dot_general(qb_ref[hd], kh, NT_DIMS, preferred_element_type=F32) * scale + bias

    def head_values(hd, alpha, p):
        vh = jnp.concatenate([vp[hd] for vp in v_pages], axis=0).astype(BF16)
        acc_ref[hd] = alpha * acc_ref[hd] + jnp.dot(p.astype(BF16), vh, preferred_element_type=F32)

    kc = min(MXU_DEPTH_V7X, x_ref.shape[1])
    n_chunks = x_ref.shape[1] // kc
    lag = 4

    def tile_and_pages(normalise):
        if normalise:
            x = x_ref[...]
            r = lax.rsqrt(jnp.mean(x * x, axis=-1, keepdims=True) + EPS)
        folded = {}
        acc = None
        for c in range(n_chunks):
            cs = slice(c * kc, (c + 1) * kc)
            if normalise:
                hc = (x_ref[:, cs] * r * g_ref[:, cs]).astype(BF16)
                h_ref[:, cs] = hc
            else:
                hc = h_ref[:, cs]
            part = jnp.dot(hc, w_ref[cs, :], preferred_element_type=F32)
            acc = part if acc is None else acc + part
            for hd in range(H):
                if hd * n_chunks // (H + lag) == c:
                    folded[hd] = softmax_step(hd, head_scores(hd))
            for hd in range(H):
                if (hd + lag) * n_chunks // (H + lag) == c:
                    head_values(hd, *folded[hd])
        o_ref[...] = acc

    first_col = pl.program_id(1) == 0
    pl.when(first_col)(lambda: tile_and_pages(True))
    pl.when(jnp.logical_not(first_col))(lambda: tile_and_pages(False))

    @pl.when(last)
    def _():
        lam = _lambda(lam_ref, lam_init)
        for hd in range(H):
            cs = slice(hd * B_HD2, (hd + 1) * B_HD2)
            s = (jnp.sum(query_rows(hd) * kn_ref[pl.ds(b, 1), cs], axis=-1, keepdims=True) * scale
                 + sb_ref[2, hd:hd + 1, 0:1])
            alpha, p = softmax_step(hd, s)
            o_both = (alpha * acc_ref[hd] + p * vn_ref[pl.ds(b, 1), cs]) * (1.0 / l_ref[hd])
            o = o_both[0:1] - lam * o_both[1:2]
            z = ps_ref[pl.ds(b, 1), width + hd * B_HD2:width + (hd + 1) * B_HD2]
            d_ref[pl.ds(b, 1), cs] = _subln_gate(o, sg_ref[...], z, lam_init)


def in_proj_with_decode(x, g, w, p_s, k_new, v_new, cache_k, cache_v, page_table, rel_bias, lam_rows, subln_g,
                        *, lam_init, tm, name):
    M, D = x.shape
    nj, _, tn = w.shape
    N = nj * tn
    B, n_pages = page_table.shape
    width = k_new.shape[-1]
    H = width // B_HD2
    k_rows = PAGE_SIZE * 2 * H
    ni = M // tm
    assert (B * n_pages) % (ni * nj) == 0, "every grid step takes the same number of pages"
    pps = B * n_pages // (ni * nj)
    assert n_pages % pps == 0 and PAGE_SIZE >= MAX_DISTANCE
    rows = 8
    table = rel_bias.astype(F32)
    dist_last = PAGE_SIZE - jnp.arange(PAGE_SIZE, dtype=jnp.int32)
    sb = jnp.stack([
        jnp.broadcast_to(table[_rel_bucket(jnp.int32(PAGE_SIZE + 1))][:, None], (H, PAGE_SIZE)),
        table[_rel_bucket(dist_last)].T,
        jnp.broadcast_to(table[_rel_bucket(jnp.int32(0))][:, None], (H, PAGE_SIZE)),
    ])
    sg = subln_g.reshape(1, B_HD2)
    whole = lambda a: pl.BlockSpec(a.shape, lambda i, j, pt: (0,) * a.ndim)
    k_page = lambda r: pl.BlockSpec((k_rows, B_HD), lambda i, j, pt: (pt[(i * nj + j) * pps + r], 0))
    v_page = lambda r: pl.BlockSpec((None, H, PAGE_SIZE, B_HD2),
                                    lambda i, j, pt: (pt[(i * nj + j) * pps + r], 0, 0, 0))
    grid_spec = pltpu.PrefetchScalarGridSpec(
        num_scalar_prefetch=1,
        grid=(ni, nj),
        in_specs=[pl.BlockSpec((tm, D), lambda i, j, pt: (i, 0), pipeline_mode=pl.Buffered(1)),
                  pl.BlockSpec((1, D), lambda i, j, pt: (0, 0)),
                  pl.BlockSpec((None, D, tn), lambda i, j, pt: (j, 0, 0)),
                  whole(p_s), whole(k_new), whole(v_new), whole(sb), whole(lam_rows), whole(sg)]
                 + [k_page(r) for r in range(pps)] + [v_page(r) for r in range(pps)],
        out_specs=[pl.BlockSpec((tm, tn), lambda i, j, pt: (i, j)),
                   pl.BlockSpec((B, width), lambda i, j, pt: (0, 0))],
        scratch_shapes=[pltpu.VMEM((tm, D), BF16), pltpu.VMEM((H, rows, B_HD2), BF16),
                        pltpu.VMEM((H, rows, 1), F32), pltpu.VMEM((H, rows, 1), F32),
                        pltpu.VMEM((H, rows, B_HD2), F32)],
    )
    return pl.pallas_call(
        functools.partial(_in_proj_decode_kernel, pps=pps, lam_init=lam_init, steps_per_seq=n_pages // pps),
        grid_spec=grid_spec,
        out_shape=[jax.ShapeDtypeStruct((M, N), F32), jax.ShapeDtypeStruct((B, width), F32)],
        compiler_params=_params("arbitrary", "arbitrary"),
        name=name,
    )(page_table.reshape(-1), x, g.reshape(1, D), w, p_s, k_new, v_new, sb, lam_rows, sg,
      *([cache_k] * pps), *([cache_v] * pps))


def _trunk(x_prompt, x_sample, mkv, smk, smv, past, W):
    bp, L, D = x_prompt.shape
    bs = x_sample.shape[0]
    mp = bp * L
    depth = W["norm_g"].shape[0]
    n_a = W["w_in_a"].shape[0]
    e_a = W["sgu_ln_g"].shape[1]
    b_width = W["w_kv"].shape[1] // 2
    mem_width = mkv.shape[-1]
    H = b_width // B_HD2
    tm, tn = 512, 1024
    tn_cast = 512
    tm_fused, tn_fused = 1024, 256
    xp = x_prompt.reshape(mp, D)
    xs = x_sample.reshape(bs, D)
    chunk_vs = []

    def mem_both(l, pp, ps, q_block):
        mo_p = mem_attention(pp.reshape(bp, L, -1), mkv, mkv, lead_k=(l, 0), lead_v=(l, 1),
                             q_block=q_block, tm=512, name=f"mem{l}_p")
        mo_s = mem_attention(ps.reshape(bs, 1, -1), smk, smv, lead_k=(l,), lead_v=(l,),
                             q_block=q_block, tm=1, name=f"mem{l}_s")
        return mo_p.reshape(mp, mem_width), mo_s.reshape(bs, mem_width)

    for l in range(depth):
        g_in = W["norm_g"][l]
        if l < n_a:
            ps, wb = rms_matmul_cast(xs, g_in, W["w_in_a"], layer=l, tn=tn_cast, name=f"in_a{l}_s")
            pp = rms_matmul(xp, g_in, wb, tm=tm, tn=tn, name=f"in_a{l}_p")
            sgu_w = (W["sgu_ln_g"][l], W["sgu_ln_b"][l], W["sgu_w"][l], W["sgu_b"][l])
            main_s, vn = sgu_first_row(ps, *sgu_w, width=e_a, name=f"sgu{l}_s")
            chunk_vs.append(vn)
            main_p = sgu_prompt(pp, *sgu_w, width=e_a, rows=2 * CHUNK, name=f"sgu{l}_p")
            q_block = 3 * e_a // mem_width
            w_out, w_layer = W["w_out_a"], l
        else:
            jb = l - n_a
            lam_init = 0.8 - 0.6 * math.exp(-0.3 * l)
            lam_rows = jnp.stack([W["lambda_q1"][jb], W["lambda_k1"][jb], W["lambda_q2"][jb], W["lambda_k2"][jb]])
            ps, wb = rms_matmul_cast(xs, g_in, W["w_in_b"], layer=jb, tn=tn_fused, column_blocked=True,
                                     name=f"in_b{jb}_s")
            pp, main_s = in_proj_with_decode(xp, g_in, wb, ps, k_s, v_s, *past, W["rel_bias"], lam_rows,
                                             W["subln_g"][jb], lam_init=lam_init, tm=tm_fused,
                                             name=f"in_b{jb}_p_dattn{jb}_s")
            main_s = main_s.astype(BF16)
            main_p = diff_attention_prompt(pp.reshape(bp, L, -1), k_att, v_p, W["rel_bias"], lam_rows,
                                           W["subln_g"][jb], lam_init=lam_init, t=512, name=f"dattn{jb}_p")
            main_p = main_p.reshape(mp, b_width)
            q_block = 2 * b_width // mem_width
            w_out, w_layer = W["w_out_b"], jb
        mo_p, mo_s = mem_both(l, pp, ps, q_block)
        xs, wb = out_proj_cast(main_s, mo_s, w_out, xs, layer=w_layer, tn=tn_cast, name=f"out{l}_s")
        if l < depth - 1:
            xp = out_proj(main_p, mo_p, wb, xp, tm=tm, tn=tn, name=f"out{l}_p")
        else:
            y_p = out_proj_norm(main_p, mo_p, wb, xp, W["final_norm_g"], tm=tm, tn=tn, name=f"out{l}_p")
        if l == n_a - 1:
            g_kv = W["kv_norm_g"]
            k_s, wb_k = rms_matmul_cast(xs, g_kv, W["w_kv"], col0=0, n_cols=b_width, tn=tn_cast, name="k_s")
            v_s, wb_v = rms_matmul_cast(xs, g_kv, W["w_kv"], col0=b_width, n_cols=b_width, tn=tn_cast, name="v_s")
            k_p, k_att = k_proj_prompt(xp, g_kv, wb_k, B=bp, tm=tm, heads=2, name="k_p")
            v_p = v_proj_prompt(xp, g_kv, wb_v, B=bp, tm=tm, heads=4, name="v_p")
    y_s = rmsnorm_rows(xs, W["final_norm_g"], tm=bs, name="final_s")
    return (y_p.reshape(bp, L, D), y_s.reshape(bs, 1, D),
            k_p.reshape(bp, L, H, 2, B_HD), jnp.transpose(v_p, (0, 2, 1, 3)),
            k_s.reshape(bs, 1, H, 2, B_HD), v_s.reshape(bs, 1, H, B_HD2), chunk_vs)


def kernel(x_prompt, x_sample, cache_k, cache_v, cache_mem_k, cache_mem_v, page_table, mem_prompt,
           norm_g, final_norm_g, mem_norm_g, w_mem_kv, w_in_a, sgu_ln_g, sgu_ln_b, sgu_w, sgu_b,
           w_out_a, kv_norm_g, w_kv, w_in_b, lambda_q1, lambda_k1, lambda_q2, lambda_k2,
           subln_g, rel_bias, w_out_b):
    depth = norm_g.shape[0]
    b_width = w_kv.shape[1] // 2
    mem_width = w_mem_kv.shape[-1] // 2
    mem_hd = mem_width // MEM_HEADS
    W = dict(norm_g=norm_g, final_norm_g=final_norm_g, sgu_ln_g=sgu_ln_g, sgu_ln_b=sgu_ln_b, sgu_w=sgu_w,
             sgu_b=sgu_b, kv_norm_g=kv_norm_g, lambda_q1=lambda_q1, lambda_k1=lambda_k1, lambda_q2=lambda_q2,
             lambda_k2=lambda_k2, subln_g=subln_g, rel_bias=rel_bias,
             w_in_a=w_in_a, w_out_a=w_out_a, w_in_b=w_in_b, w_out_b=w_out_b, w_kv=w_kv)

    bd, dec_seq = x_sample.shape[:2]
    assert dec_seq == 1, "the sample group is one new token per sequence"
    past = (cache_k.reshape(-1, B_HD), jnp.transpose(cache_v, (0, 2, 1, 3)), page_table)
    smk = cache_mem_k.reshape(depth, bd, -1, mem_width)
    smv = cache_mem_v.reshape(depth, bd, -1, mem_width)

    bp, m_len, d_model = mem_prompt.shape
    mkv = rms_matmul_stacked(mem_prompt.reshape(bp * m_len, d_model), mem_norm_g, w_mem_kv.astype(BF16),
                             tm=512, tn=mem_width, name="mem_kv")
    mkv = mkv.reshape(depth, 2, bp, m_len, mem_width)

    y_p, y_s, k_p, v_p, k_s, v_s, chunk_vs = _trunk(x_prompt, x_sample, mkv, smk, smv, past, W)
    chunk_v_s = jnp.stack(chunk_vs).reshape(len(chunk_vs), bd, dec_seq, -1)
    mem_shape = (depth, bp, m_len, MEM_HEADS, mem_hd)
    return (y_p, y_s, k_p, v_p, k_s, v_s, mkv[:, 0].reshape(mem_shape), mkv[:, 1].reshape(mem_shape), chunk_v_s)
```

```python
import functools
import math

import jax
import jax.numpy as jnp
from jax import lax
from jax.experimental import pallas as pl
from jax.experimental.pallas import tpu as pltpu

F32 = jnp.float32
BF16 = jnp.bfloat16

CHUNK = 128
SGU_GROUPS = 8
B_HD2 = 256
B_HD = B_HD2 // 2
MEM_HEADS = 4
N_BUCKETS = 32
MAX_DISTANCE = 128
PAGE_SIZE = 128
EPS = 1e-6
NEG = -1e30

VMEM_BYTES_V7X = 64 * 1024 * 1024
MXU_DEPTH_V7X = 256
VMEM_LIMIT = VMEM_BYTES_V7X - 8 * 1024 * 1024

NT_DIMS = (((1,), (1,)), ((), ()))


def _params(*sem):
    return pltpu.CompilerParams(dimension_semantics=sem, vmem_limit_bytes=VMEM_LIMIT)


def _normalised_rows(x_ref, g_ref, h_ref):
    @pl.when(pl.program_id(1) == 0)
    def _():
        x = x_ref[...]
        ms = jnp.mean(x * x, axis=-1, keepdims=True)
        h_ref[...] = (x * lax.rsqrt(ms + EPS) * g_ref[...]).astype(BF16)

    return h_ref[...]


def _rms_dot(x_ref, g_ref, w_ref, h_ref, emit):
    first = pl.program_id(1) == 0

    @pl.when(first)
    def _():
        x = x_ref[...]
        r = lax.rsqrt(jnp.mean(x * x, axis=-1, keepdims=True) + EPS)
        kc = min(MXU_DEPTH_V7X, x_ref.shape[1])
        acc = None
        for c in range(x_ref.shape[1] // kc):
            cs = slice(c * kc, (c + 1) * kc)
            hc = (x_ref[:, cs] * r * g_ref[:, cs]).astype(BF16)
            h_ref[:, cs] = hc
            part = jnp.dot(hc, w_ref[cs, :], preferred_element_type=F32)
            acc = part if acc is None else acc + part
        emit(acc)

    @pl.when(jnp.logical_not(first))
    def _():
        emit(jnp.dot(h_ref[...], w_ref[...], preferred_element_type=F32))


def _rms_matmul_kernel(x_ref, g_ref, w_ref, o_ref, h_ref):
    def emit(res):
        o_ref[...] = res.astype(o_ref.dtype)

    _rms_dot(x_ref, g_ref, w_ref, h_ref, emit)


def rms_matmul(x, g, w, *, tm, tn, name, out_dtype=F32):
    M, D = x.shape
    N = w.shape[1]
    return pl.pallas_call(
        _rms_matmul_kernel,
        grid=(M // tm, N // tn),
        in_specs=[pl.BlockSpec((tm, D), lambda i, j: (i, 0)),
                  pl.BlockSpec((1, D), lambda i, j: (0, 0)),
                  pl.BlockSpec((D, tn), lambda i, j: (0, j))],
        out_specs=pl.BlockSpec((tm, tn), lambda i, j: (i, j)),
        out_shape=jax.ShapeDtypeStruct((M, N), out_dtype),
        scratch_shapes=[pltpu.VMEM((tm, D), BF16)],
        compiler_params=_params("parallel", "arbitrary"),
        name=name,
    )(x, g.reshape(1, D), w)


def _rms_matmul_cast_kernel(x_ref, g_ref, w_ref, o_ref, wb_ref, h_ref):
    h = _normalised_rows(x_ref, g_ref, h_ref)
    wb = w_ref[...].astype(BF16)
    wb_ref[...] = wb
    o_ref[...] = jnp.dot(h, wb, preferred_element_type=F32)


def _f32_weight_spec(w, layer, tn, first_block=0):
    if w.ndim == 2:
        return pl.BlockSpec((w.shape[0], tn), lambda i, j: (0, first_block + j))
    return pl.BlockSpec((None, w.shape[1], tn), lambda i, j: (layer, 0, first_block + j))


def rms_matmul_cast(x, g, w, *, tn, name, layer=0, col0=0, n_cols=None, column_blocked=False):
    M, D = x.shape
    n_cols = w.shape[-1] if n_cols is None else n_cols
    if column_blocked:
        wb_spec = pl.BlockSpec((None, D, tn), lambda i, j: (j, 0, 0))
        wb_shape = jax.ShapeDtypeStruct((n_cols // tn, D, tn), BF16)
    else:
        wb_spec = pl.BlockSpec((D, tn), lambda i, j: (0, j))
        wb_shape = jax.ShapeDtypeStruct((D, n_cols), BF16)
    return pl.pallas_call(
        _rms_matmul_cast_kernel,
        grid=(1, n_cols // tn),
        in_specs=[pl.BlockSpec((M, D), lambda i, j: (0, 0)),
                  pl.BlockSpec((1, D), lambda i, j: (0, 0)),
                  _f32_weight_spec(w, layer, tn, col0 // tn)],
        out_specs=[pl.BlockSpec((M, tn), lambda i, j: (0, j)), wb_spec],
        out_shape=[jax.ShapeDtypeStruct((M, n_cols), F32), wb_shape],
        scratch_shapes=[pltpu.VMEM((M, D), BF16)],
        compiler_params=_params("arbitrary", "arbitrary"),
        name=name,
    )(x, g.reshape(1, D), w)


def rms_matmul_stacked(x, g, w, *, tm, tn, name):
    M, D = x.shape
    G, _, N = w.shape
    nb = N // tn
    return pl.pallas_call(
        _rms_matmul_kernel,
        grid=(M // tm, G * nb),
        in_specs=[pl.BlockSpec((tm, D), lambda i, j: (i, 0)),
                  pl.BlockSpec((1, D), lambda i, j: (0, 0)),
                  pl.BlockSpec((None, D, tn), lambda i, j: (j // nb, 0, j % nb))],
        out_specs=pl.BlockSpec((None, tm, tn), lambda i, j: (j, i, 0)),
        out_shape=jax.ShapeDtypeStruct((G * nb, M, tn), F32),
        scratch_shapes=[pltpu.VMEM((tm, D), BF16)],
        compiler_params=_params("parallel", "arbitrary"),
        name=name,
    )(x, g.reshape(1, D), w)


def _k_proj_kernel(x_ref, g_ref, w_ref, knat_ref, katt_ref, h_ref):
    tm = x_ref.shape[0]
    heads = katt_ref.shape[0]
    j = pl.program_id(1)
    n_sub = knat_ref.shape[0] // tm

    def emit(res):
        for q in range(2 * heads):
            knat_ref[pl.ds(j * 2 * heads + q, tm, stride=n_sub), :] = res[:, q * B_HD:(q + 1) * B_HD]
        for hh in range(heads):
            katt_ref[hh] = res[:, hh * B_HD2:(hh + 1) * B_HD2].astype(BF16)

    _rms_dot(x_ref, g_ref, w_ref, h_ref, emit)


def k_proj_prompt(x, g, w, *, B, tm, heads, name):
    M, D = x.shape
    L = M // B
    N = w.shape[1]
    H = N // B_HD2
    nlb = L // tm
    return pl.pallas_call(
        _k_proj_kernel,
        grid=(M // tm, H // heads),
        in_specs=[pl.BlockSpec((tm, D), lambda i, j: (i, 0), pipeline_mode=pl.Buffered(1)),
                  pl.BlockSpec((1, D), lambda i, j: (0, 0)),
                  pl.BlockSpec((D, heads * B_HD2), lambda i, j: (0, j))],
        out_specs=[pl.BlockSpec((tm * 2 * H, B_HD), lambda i, j: (i, 0)),
                   pl.BlockSpec((None, heads, tm, B_HD2), lambda i, j: (i // nlb, j, i % nlb, 0))],
        out_shape=[jax.ShapeDtypeStruct((M * 2 * H, B_HD), F32),
                   jax.ShapeDtypeStruct((B, H, L, B_HD2), BF16)],
        scratch_shapes=[pltpu.VMEM((tm, D), BF16)],
        compiler_params=_params("parallel", "arbitrary"),
        name=name,
    )(x, g.reshape(1, D), w)


def _v_proj_kernel(x_ref, g_ref, w_ref, v_ref, h_ref):
    def emit(res):
        for hh in range(v_ref.shape[0]):
            v_ref[hh] = res[:, hh * B_HD2:(hh + 1) * B_HD2]

    _rms_dot(x_ref, g_ref, w_ref, h_ref, emit)


def v_proj_prompt(x, g, w, *, B, tm, heads, name):
    M, D = x.shape
    L = M // B
    H = w.shape[1] // B_HD2
    nlb = L // tm
    return pl.pallas_call(
        _v_proj_kernel,
        grid=(M // tm, H // heads),
        in_specs=[pl.BlockSpec((tm, D), lambda i, j: (i, 0)),
                  pl.BlockSpec((1, D), lambda i, j: (0, 0)),
                  pl.BlockSpec((D, heads * B_HD2), lambda i, j: (0, j))],
        out_specs=pl.BlockSpec((None, heads, tm, B_HD2), lambda i, j: (i // nlb, j, i % nlb, 0)),
        out_shape=jax.ShapeDtypeStruct((B, H, L, B_HD2), F32),
        scratch_shapes=[pltpu.VMEM((tm, D), BF16)],
        compiler_params=_params("parallel", "arbitrary"),
        name=name,
    )(x, g.reshape(1, D), w)


def _out_proj_kernel(a_ref, m_ref, w_ref, x_ref, o_ref):
    ka = a_ref.shape[1]
    acc = jnp.dot(a_ref[...], w_ref[:ka, :], preferred_element_type=F32)
    acc = acc + jnp.dot(m_ref[...], w_ref[ka:, :], preferred_element_type=F32)
    o_ref[...] = x_ref[...] + acc


def out_proj(a, m, w, x, *, tm, tn, name):
    M, ka = a.shape
    km = m.shape[1]
    N = w.shape[1]
    return pl.pallas_call(
        _out_proj_kernel,
        grid=(M // tm, N // tn),
        in_specs=[pl.BlockSpec((tm, ka), lambda i, j: (i, 0)),
                  pl.BlockSpec((tm, km), lambda i, j: (i, 0)),
                  pl.BlockSpec((ka + km, tn), lambda i, j: (0, j)),
                  pl.BlockSpec((tm, tn), lambda i, j: (i, j))],
        out_specs=pl.BlockSpec((tm, tn), lambda i, j: (i, j)),
        out_shape=jax.ShapeDtypeStruct((M, N), F32),
        compiler_params=_params("parallel", "arbitrary"),
        name=name,
    )(a, m, w, x)


def _out_proj_cast_kernel(a_ref, m_ref, w_ref, x_ref, o_ref, wb_ref):
    ka = a_ref.shape[1]
    wb = w_ref[...].astype(BF16)
    wb_ref[...] = wb
    acc = jnp.dot(a_ref[...], wb[:ka], preferred_element_type=F32)
    acc = acc + jnp.dot(m_ref[...], wb[ka:], preferred_element_type=F32)
    o_ref[...] = x_ref[...] + acc


def out_proj_cast(a, m, w, x, *, tn, name, layer=0):
    M, ka = a.shape
    km = m.shape[1]
    N = w.shape[-1]
    return pl.pallas_call(
        _out_proj_cast_kernel,
        grid=(1, N // tn),
        in_specs=[pl.BlockSpec((M, ka), lambda i, j: (0, 0)),
                  pl.BlockSpec((M, km), lambda i, j: (0, 0)),
                  _f32_weight_spec(w, layer, tn),
                  pl.BlockSpec((M, tn), lambda i, j: (0, j))],
        out_specs=[pl.BlockSpec((M, tn), lambda i, j: (0, j)),
                   pl.BlockSpec((ka + km, tn), lambda i, j: (0, j))],
        out_shape=[jax.ShapeDtypeStruct((M, N), F32),
                   jax.ShapeDtypeStruct((ka + km, N), BF16)],
        compiler_params=_params("arbitrary", "arbitrary"),
        name=name,
    )(a, m, w, x)


def _out_proj_norm_kernel(a_ref, m_ref, w_ref, x_ref, g_ref, o_ref):
    j = pl.program_id(1)
    ka = a_ref.shape[1]
    tn = x_ref.shape[1]
    acc = jnp.dot(a_ref[...], w_ref[:ka, :], preferred_element_type=F32)
    acc = acc + jnp.dot(m_ref[...], w_ref[ka:, :], preferred_element_type=F32)
    o_ref[:, pl.ds(pl.multiple_of(j * tn, tn), tn)] = x_ref[...] + acc

    @pl.when(j == pl.num_programs(1) - 1)
    def _():
        y = o_ref[...]
        ms = jnp.mean(y * y, axis=-1, keepdims=True)
        o_ref[...] = y * lax.rsqrt(ms + EPS) * g_ref[...]


def out_proj_norm(a, m, w, x, g, *, tm, tn, name):
    M, ka = a.shape
    km = m.shape[1]
    N = w.shape[-1]
    return pl.pallas_call(
        _out_proj_norm_kernel,
        grid=(M // tm, N // tn),
        in_specs=[pl.BlockSpec((tm, ka), lambda i, j: (i, 0)),
                  pl.BlockSpec((tm, km), lambda i, j: (i, 0)),
                  pl.BlockSpec((ka + km, tn), lambda i, j: (0, j)),
                  pl.BlockSpec((tm, tn), lambda i, j: (i, j)),
                  pl.BlockSpec((1, N), lambda i, j: (0, 0))],
        out_specs=pl.BlockSpec((tm, N), lambda i, j: (i, 0)),
        out_shape=jax.ShapeDtypeStruct((M, N), F32),
        compiler_params=_params("parallel", "arbitrary"),
        name=name,
    )(a, m, w, x, g.reshape(1, N))


def _rmsnorm_kernel(x_ref, g_ref, o_ref):
    x = x_ref[...]
    ms = jnp.mean(x * x, axis=-1, keepdims=True)
    o_ref[...] = x * lax.rsqrt(ms + EPS) * g_ref[...]


def rmsnorm_rows(x, g, *, tm, name):
    M, D = x.shape
    tm = min(tm, M)
    return pl.pallas_call(
        _rmsnorm_kernel,
        grid=(M // tm,),
        in_specs=[pl.BlockSpec((tm, D), lambda i: (i, 0)),
                  pl.BlockSpec((1, D), lambda i: (0, 0))],
        out_specs=pl.BlockSpec((tm, D), lambda i: (i, 0)),
        out_shape=jax.ShapeDtypeStruct((M, D), F32),
        compiler_params=_params("parallel"),
        name=name,
    )(x, g.reshape(1, D))


def _layernorm(v, g, b):
    mu = jnp.mean(v, axis=-1, keepdims=True)
    vc = v - mu
    var = jnp.mean(vc * vc, axis=-1, keepdims=True)
    return vc * lax.rsqrt(var + EPS) * g + b


def _sgu_kernel(u_ref, v_ref, z_ref, lng_ref, lnb_ref, w_ref, bt_ref, o_ref):
    rows, width = u_ref.shape
    ch = width // SGU_GROUPS
    r = lax.broadcasted_iota(jnp.int32, (CHUNK, CHUNK), 0)
    c = lax.broadcasted_iota(jnp.int32, (CHUNK, CHUNK), 1)
    causal = c <= r
    ws = [jnp.where(causal, w_ref[g], 0.0).astype(BF16) for g in range(SGU_GROUPS)]
    for n in range(rows // CHUNK):
        rs = slice(n * CHUNK, (n + 1) * CHUNK)
        vn = _layernorm(jax.nn.gelu(v_ref[rs, :].astype(F32)), lng_ref[...], lnb_ref[...]).astype(BF16)
        for g in range(SGU_GROUPS):
            cs = slice(g * ch, (g + 1) * ch)
            mixed = jnp.dot(ws[g], vn[:, cs], preferred_element_type=F32) + bt_ref[:, g:g + 1]
            u = jax.nn.gelu(u_ref[rs, cs].astype(F32))
            o_ref[rs, cs] = (u * mixed * jax.nn.silu(z_ref[rs, cs].astype(F32))).astype(o_ref.dtype)


def sgu_prompt(p, ln_g, ln_b, w_s, b_s, *, width, rows, name):
    M = p.shape[0]
    return pl.pallas_call(
        _sgu_kernel,
        grid=(M // rows,),
        in_specs=[pl.BlockSpec((rows, width), lambda i: (i, 0)),
                  pl.BlockSpec((rows, width), lambda i: (i, 1)),
                  pl.BlockSpec((rows, width), lambda i: (i, 2)),
                  pl.BlockSpec((1, width), lambda i: (0, 0)),
                  pl.BlockSpec((1, width), lambda i: (0, 0)),
                  pl.BlockSpec((SGU_GROUPS, CHUNK, CHUNK), lambda i: (0, 0, 0)),
                  pl.BlockSpec((CHUNK, SGU_GROUPS), lambda i: (0, 0))],
        out_specs=pl.BlockSpec((rows, width), lambda i: (i, 0)),
        out_shape=jax.ShapeDtypeStruct((M, width), BF16),
        compiler_params=_params("parallel"),
        name=name,
    )(p, p, p, ln_g.reshape(1, width), ln_b.reshape(1, width), w_s, b_s.T)


def _sgu_first_row_kernel(u_ref, v_ref, z_ref, lng_ref, lnb_ref, w0_ref, b0_ref, o_ref, vn_ref):
    vn = _layernorm(jax.nn.gelu(v_ref[...]), lng_ref[...], lnb_ref[...])
    vn_ref[...] = vn
    mixed = w0_ref[...] * vn + b0_ref[...]
    o_ref[...] = (jax.nn.gelu(u_ref[...]) * mixed * jax.nn.silu(z_ref[...])).astype(o_ref.dtype)


def sgu_first_row(p, ln_g, ln_b, w_s, b_s, *, width, name):
    M = p.shape[0]
    ch = width // SGU_GROUPS
    w0 = jnp.repeat(w_s[:, 0, 0], ch).reshape(1, width)
    b0 = jnp.repeat(b_s[:, 0], ch).reshape(1, width)
    row = pl.BlockSpec((1, width), lambda i: (0, 0))
    return pl.pallas_call(
        _sgu_first_row_kernel,
        grid=(1,),
        in_specs=[pl.BlockSpec((M, width), lambda i: (0, 0)),
                  pl.BlockSpec((M, width), lambda i: (0, 1)),
                  pl.BlockSpec((M, width), lambda i: (0, 2)),
                  row, row, row, row],
        out_specs=[pl.BlockSpec((M, width), lambda i: (0, 0)),
                   pl.BlockSpec((M, width), lambda i: (0, 0))],
        out_shape=[jax.ShapeDtypeStruct((M, width), BF16),
                   jax.ShapeDtypeStruct((M, width), F32)],
        compiler_params=_params("arbitrary"),
        name=name,
    )(p, p, p, ln_g.reshape(1, width), ln_b.reshape(1, width), w0, b0)


def _mem_attn_kernel(q_ref, zm_ref, k_ref, v_ref, o_ref):
    tm, width = q_ref.shape
    hd = width // MEM_HEADS
    q = q_ref[...]
    if tm < 8:
        q = jnp.broadcast_to(q[0:1], (8, width))
    for h in range(MEM_HEADS):
        cs = slice(h * hd, (h + 1) * hd)
        s = lax.dot_general(q[:, cs].astype(BF16), k_ref[:, cs].astype(BF16), NT_DIMS,
                            preferred_element_type=F32) * (hd ** -0.5)
        e = jnp.exp(s - jnp.max(s, axis=-1, keepdims=True))
        p = e * (1.0 / jnp.sum(e, axis=-1, keepdims=True))
        o = jnp.dot(p.astype(BF16), v_ref[:, cs].astype(BF16), preferred_element_type=F32)
        o_ref[:, cs] = (o[:tm] * jax.nn.silu(zm_ref[:, cs].astype(F32))).astype(o_ref.dtype)


def mem_attention(p, mem_k, mem_v, *, lead_k, lead_v, q_block, tm, name):
    B, L, _ = p.shape
    mlen, width = mem_k.shape[-2:]
    tm = min(tm, L)
    mem_spec = lambda lead: pl.BlockSpec((None,) * (len(lead) + 1) + (mlen, width),
                                         lambda b, i: (*lead, b, 0, 0))
    return pl.pallas_call(
        _mem_attn_kernel,
        grid=(B, L // tm),
        in_specs=[pl.BlockSpec((None, tm, width), lambda b, i: (b, i, q_block)),
                  pl.BlockSpec((None, tm, width), lambda b, i: (b, i, q_block + 1)),
                  mem_spec(lead_k), mem_spec(lead_v)],
        out_specs=pl.BlockSpec((None, tm, width), lambda b, i: (b, i, 0)),
        out_shape=jax.ShapeDtypeStruct((B, L, width), BF16),
        compiler_params=_params("parallel", "parallel"),
        name=name,
    )(p, p, mem_k, mem_v)


def _rel_bucket(dist):
    n = jnp.maximum(dist, 0)
    max_exact = N_BUCKETS // 2
    nf = jnp.maximum(n, 1).astype(F32)
    large = max_exact + (jnp.log(nf / max_exact) / math.log(MAX_DISTANCE / max_exact)
                         * (N_BUCKETS - max_exact)).astype(jnp.int32)
    large = jnp.minimum(large, N_BUCKETS - 1)
    return jnp.where(n < max_exact, n, large)


def _lambda(lam_ref, lam_init):
    a = jnp.sum(lam_ref[0:1, :] * lam_ref[1:2, :], axis=-1, keepdims=True)
    b = jnp.sum(lam_ref[2:3, :] * lam_ref[3:4, :], axis=-1, keepdims=True)
    return jnp.exp(a) - jnp.exp(b) + lam_init


def _subln_gate(o, g, z, lam_init):
    ms = jnp.mean(o * o, axis=-1, keepdims=True)
    on = (o * lax.rsqrt(ms + EPS) * g) * (1.0 - lam_init)
    return on * jax.nn.silu(z)


def _lane_groups(x):
    return [x[:, g * 128:(g + 1) * 128] for g in range(x.shape[1] // 128)]


def _diff_attn_kernel(q_ref, z_ref, k_ref, v_ref, bd_ref, be_ref, bfar_ref, lam_ref, g_ref, o_ref,
                      vb_ref, b0_ref, b1_ref, s_ref, mp_ref, lp_ref, acc_ref, *, lam_init):
    t = q_ref.shape[0]
    qi = pl.program_id(2)
    scale = B_HD ** -0.5
    md = MAX_DISTANCE
    nb = t // md

    @pl.when(qi == 0)
    def _():
        vb_ref[...] = v_ref[...].astype(BF16)
        far = jnp.broadcast_to(bfar_ref[0:1, 0:1], (md, md))
        masked = jnp.full((md, md), NEG, F32)
        near, next_to_near = bd_ref[...], be_ref[...]
        for bi in range(nb):
            for bj in range(nb):
                piece = near if bi == bj else next_to_near if bi == bj + 1 else far if bi > bj else masked
                b0_ref[bi * md:(bi + 1) * md, bj * md:(bj + 1) * md] = piece
                piece = next_to_near if (bi == 0 and bj == nb - 1) else far
                b1_ref[bi * md:(bi + 1) * md, bj * md:(bj + 1) * md] = piece

    q = q_ref[...]
    qs = (q[:, :B_HD].astype(BF16), q[:, B_HD:].astype(BF16))
    mp_ref[...] = jnp.full(mp_ref.shape, NEG, F32)
    lp_ref[...] = jnp.zeros(lp_ref.shape, F32)
    acc_ref[...] = jnp.zeros(acc_ref.shape, F32)

    def scores(kj, bias):
        off = pl.multiple_of(kj * t, t)
        k = k_ref[pl.ds(off, t), :]
        for c in range(2):
            s = lax.dot_general(qs[c], k[:, c * B_HD:(c + 1) * B_HD], NT_DIMS,
                                preferred_element_type=F32) * scale + bias
            s_ref[c, :, pl.ds(off, t)] = s
            mp_ref[c] = functools.reduce(jnp.maximum, _lane_groups(s), mp_ref[c])

    bfar = bfar_ref[0:1, 0:1]

    def far(kj, carry):
        scores(kj, bfar)
        return carry

    lax.fori_loop(0, jnp.maximum(qi - 1, 0), far, 0)

    @pl.when(qi >= 1)
    def _():
        scores(qi - 1, b1_ref[...])

    scores(qi, b0_ref[...])

    m_rows = [jnp.broadcast_to(jnp.max(mp_ref[c], axis=-1, keepdims=True), (t, 128)) for c in range(2)]

    def accumulate(kj, carry):
        off = pl.multiple_of(kj * t, t)
        v = vb_ref[pl.ds(off, t), :]
        for c in range(2):
            ps = [jnp.exp(sg - m_rows[c]) for sg in _lane_groups(s_ref[c, :, pl.ds(off, t)])]
            lp_ref[c] = functools.reduce(jnp.add, ps, lp_ref[c])
            p = jnp.concatenate(ps, axis=1).astype(BF16)
            acc_ref[c] = acc_ref[c] + jnp.dot(p, v, preferred_element_type=F32)
        return carry

    lax.fori_loop(0, qi + 1, accumulate, 0)

    lam = _lambda(lam_ref, lam_init)
    inv_l = [1.0 / jnp.sum(lp_ref[c], axis=-1, keepdims=True) for c in range(2)]
    o = acc_ref[0] * inv_l[0] - lam * (acc_ref[1] * inv_l[1])
    o_ref[...] = _subln_gate(o, g_ref[...], z_ref[...], lam_init).astype(o_ref.dtype)


def _toeplitz_tiles(g, t):
    H, P = g.shape
    hankel = jnp.tile(g[:, ::-1], (1, t + 1))[:, :t * (P + 1)].reshape(H, t, P + 1)
    return hankel[:, ::-1, :t]


def diff_attention_prompt(p, k, v, rel_bias, lam_rows, subln_g, *, lam_init, t, name):
    B, L, _ = p.shape
    H = k.shape[1]
    md = MAX_DISTANCE
    assert t % md == 0 and L % t == 0
    table = rel_bias.astype(F32)
    d = jnp.arange(-(md - 1), md, dtype=jnp.int32)
    bd = _toeplitz_tiles(jnp.where((d >= 0)[None], table[_rel_bucket(d)].T, NEG), md)
    be = _toeplitz_tiles(table[_rel_bucket(d + md)].T, md)
    bfar = jnp.broadcast_to(table[_rel_bucket(jnp.int32(md + 1))][:, None, None], (H, 1, 128))
    blk = lambda off: pl.BlockSpec((None, t, B_HD2), lambda b, h, i: (b, i, off + h))
    kv = pl.BlockSpec((None, None, L, B_HD2), lambda b, h, i: (b, h, 0, 0))
    tile = pl.BlockSpec((None, md, md), lambda b, h, i: (h, 0, 0))
    return pl.pallas_call(
        functools.partial(_diff_attn_kernel, lam_init=lam_init),
        grid=(B, H, L // t),
        in_specs=[blk(0), blk(H), kv, kv, tile, tile,
                  pl.BlockSpec((None, 1, 128), lambda b, h, i: (h, 0, 0)),
                  pl.BlockSpec((4, B_HD), lambda b, h, i: (0, 0)),
                  pl.BlockSpec((1, B_HD2), lambda b, h, i: (0, 0))],
        out_specs=pl.BlockSpec((None, t, B_HD2), lambda b, h, i: (b, i, h)),
        out_shape=jax.ShapeDtypeStruct((B, L, H * B_HD2), BF16),
        scratch_shapes=[pltpu.VMEM((L, B_HD2), BF16), pltpu.VMEM((t, t), F32), pltpu.VMEM((t, t), F32),
                        pltpu.VMEM((2, t, L), F32), pltpu.VMEM((2, t, 128), F32), pltpu.VMEM((2, t, 128), F32),
                        pltpu.VMEM((2, t, B_HD2), F32)],
        compiler_params=_params("parallel", "parallel", "arbitrary"),
        name=name,
    )(p, p, k, v, bd, be, bfar, lam_rows, subln_g.reshape(1, B_HD2))


def _in_proj_decode_kernel(pt_ref, x_ref, g_ref, w_ref, ps_ref, kn_ref, vn_ref, sb_ref, lam_ref, sg_ref, *rest,
                           pps, lam_init, steps_per_seq):
    del pt_ref
    k_pages, v_pages = rest[:pps], rest[pps:2 * pps]
    o_ref, d_ref, h_ref, qb_ref, m_ref, l_ref, acc_ref = rest[2 * pps:]
    H, rows, _ = qb_ref.shape
    width = H * B_HD2
    step = pl.program_id(0) * pl.num_programs(1) + pl.program_id(1)
    b = step // steps_per_seq
    first = step % steps_per_seq == 0
    last = step % steps_per_seq == steps_per_seq - 1
    scale = B_HD ** -0.5

    def query_rows(hd):
        r_id = lax.broadcasted_iota(jnp.int32, (rows, B_HD2), 0)
        c_id = lax.broadcasted_iota(jnp.int32, (rows, B_HD2), 1)
        own = (c_id >> 7) == r_id
        return jnp.where(own, ps_ref[pl.ds(b, 1), hd * B_HD2:(hd + 1) * B_HD2], 0.0)

    @pl.when(first)
    def _():
        for hd in range(H):
            qb_ref[hd] = query_rows(hd).astype(BF16)
        m_ref[...] = jnp.full(m_ref.shape, NEG, F32)
        l_ref[...] = jnp.zeros(l_ref.shape, F32)
        acc_ref[...] = jnp.zeros(acc_ref.shape, F32)

    def softmax_step(hd, s):
        m_old = m_ref[hd]
        m_new = jnp.maximum(m_old, jnp.max(s, axis=-1, keepdims=True))
        alpha = jnp.exp(m_old - m_new)
        p = jnp.exp(s - m_new)
        l_ref[hd] = alpha * l_ref[hd] + jnp.sum(p, axis=-1, keepdims=True)
        m_ref[hd] = m_new
        return alpha, p

    def head_scores(hd):
        kh = jnp.concatenate(
            [jnp.concatenate([kp[pl.ds(2 * hd + c, PAGE_SIZE, stride=2 * H), :] for c in range(2)], axis=1)
             for kp in k_pages], axis=0).astype(BF16)
        bias = jnp.concatenate([sb_ref[0, hd:hd + 1, :]] * (pps - 1)
                               + [jnp.where(last, sb_ref[1, hd:hd + 1, :], sb_ref[0, hd:hd + 1, :])], axis=1)
        return lax.dot_general(qb_ref[hd], kh, NT_DIMS, preferred_element_type=F32) * scale + bias

    def head_values(hd, alpha, p):
        vh = jnp.concatenate([vp[hd] for vp in v_pages], axis=0).astype(BF16)
        acc_ref[hd] = alpha * acc_ref[hd] + jnp.dot(p.astype(BF16), vh, preferred_element_type=F32)

    kc = min(MXU_DEPTH_V7X, x_ref.shape[1])
    n_chunks = x_ref.shape[1] // kc
    lag = 4

    def tile_and_pages(normalise):
        if normalise:
            x = x_ref[...]
            r = lax.rsqrt(jnp.mean(x * x, axis=-1, keepdims=True) + EPS)
        folded = {}
        acc = None
        for c in range(n_chunks):
            cs = slice(c * kc, (c + 1) * kc)
            if normalise:
                hc = (x_ref[:, cs] * r * g_ref[:, cs]).astype(BF16)
                h_ref[:, cs] = hc
            else:
                hc = h_ref[:, cs]
            part = jnp.dot(hc, w_ref[cs, :], preferred_element_type=F32)
            acc = part if acc is None else acc + part
            for hd in range(H):
                if hd * n_chunks // (H + lag) == c:
                    folded[hd] = softmax_step(hd, head_scores(hd))
            for hd in range(H):
                if (hd + lag) * n_chunks // (H + lag) == c:
                    head_values(hd, *folded[hd])
        o_ref[...] = acc

    first_col = pl.program_id(1) == 0
    pl.when(first_col)(lambda: tile_and_pages(True))
    pl.when(jnp.logical_not(first_col))(lambda: tile_and_pages(False))

    @pl.when(last)
    def _():
        lam = _lambda(lam_ref, lam_init)
        for hd in range(H):
            cs = slice(hd * B_HD2, (hd + 1) * B_HD2)
            s = (jnp.sum(query_rows(hd) * kn_ref[pl.ds(b, 1), cs], axis=-1, keepdims=True) * scale
                 + sb_ref[2, hd:hd + 1, 0:1])
            alpha, p = softmax_step(hd, s)
            o_both = (alpha * acc_ref[hd] + p * vn_ref[pl.ds(b, 1), cs]) * (1.0 / l_ref[hd])
            o = o_both[0:1] - lam * o_both[1:2]
            z = ps_ref[pl.ds(b, 1), width + hd * B_HD2:width + (hd + 1) * B_HD2]
            d_ref[pl.ds(b, 1), cs] = _subln_gate(o, sg_ref[...], z, lam_init)


def in_proj_with_decode(x, g, w, p_s, k_new, v_new, cache_k, cache_v, page_table, rel_bias, lam_rows, subln_g,
                        *, lam_init, tm, name):
    M, D = x.shape
    nj, _, tn = w.shape
    N = nj * tn
    B, n_pages = page_table.shape
    width = k_new.shape[-1]
    H = width // B_HD2
    k_rows = PAGE_SIZE * 2 * H
    ni = M // tm
    assert (B * n_pages) % (ni * nj) == 0, "every grid step takes the same number of pages"
    pps = B * n_pages // (ni * nj)
    assert n_pages % pps == 0 and PAGE_SIZE >= MAX_DISTANCE
    rows = 8
    table = rel_bias.astype(F32)
    dist_last = PAGE_SIZE - jnp.arange(PAGE_SIZE, dtype=jnp.int32)
    sb = jnp.stack([
        jnp.broadcast_to(table[_rel_bucket(jnp.int32(PAGE_SIZE + 1))][:, None], (H, PAGE_SIZE)),
        table[_rel_bucket(dist_last)].T,
        jnp.broadcast_to(table[_rel_bucket(jnp.int32(0))][:, None], (H, PAGE_SIZE)),
    ])
    sg = subln_g.reshape(1, B_HD2)
    whole = lambda a: pl.BlockSpec(a.shape, lambda i, j, pt: (0,) * a.ndim)
    k_page = lambda r: pl.BlockSpec((k_rows, B_HD), lambda i, j, pt: (pt[(i * nj + j) * pps + r], 0))
    v_page = lambda r: pl.BlockSpec((None, H, PAGE_SIZE, B_HD2),
                                    lambda i, j, pt: (pt[(i * nj + j) * pps + r], 0, 0, 0))
    grid_spec = pltpu.PrefetchScalarGridSpec(
        num_scalar_prefetch=1,
        grid=(ni, nj),
        in_specs=[pl.BlockSpec((tm, D), lambda i, j, pt: (i, 0), pipeline_mode=pl.Buffered(1)),
                  pl.BlockSpec((1, D), lambda i, j, pt: (0, 0)),
                  pl.BlockSpec((None, D, tn), lambda i, j, pt: (j, 0, 0)),
                  whole(p_s), whole(k_new), whole(v_new), whole(sb), whole(lam_rows), whole(sg)]
                 + [k_page(r) for r in range(pps)] + [v_page(r) for r in range(pps)],
        out_specs=[pl.BlockSpec((tm, tn), lambda i, j, pt: (i, j)),
                   pl.BlockSpec((B, width), lambda i, j, pt: (0, 0))],
        scratch_shapes=[pltpu.VMEM((tm, D), BF16), pltpu.VMEM((H, rows, B_HD2), BF16),
                        pltpu.VMEM((H, rows, 1), F32), pltpu.VMEM((H, rows, 1), F32),
                        pltpu.VMEM((H, rows, B_HD2), F32)],
    )
    return pl.pallas_call(
        functools.partial(_in_proj_decode_kernel, pps=pps, lam_init=lam_init, steps_per_seq=n_pages // pps),
        grid_spec=grid_spec,
        out_shape=[jax.ShapeDtypeStruct((M, N), F32), jax.ShapeDtypeStruct((B, width), F32)],
        compiler_params=_params("arbitrary", "arbitrary"),
        name=name,
    )(page_table.reshape(-1), x, g.reshape(1, D), w, p_s, k_new, v_new, sb, lam_rows, sg,
      *([cache_k] * pps), *([cache_v] * pps))


def _trunk(x_prompt, x_sample, mkv, smk, smv, past, W):
    bp, L, D = x_prompt.shape
    bs = x_sample.shape[0]
    mp = bp * L
    depth = W["norm_g"].shape[0]
    n_a = W["w_in_a"].shape[0]
    e_a = W["sgu_ln_g"].shape[1]
    b_width = W["w_kv"].shape[1] // 2
    mem_width = mkv.shape[-1]
    H = b_width // B_HD2
    tm, tn = 512, 1024
    tn_cast = 512
    tm_fused, tn_fused = 1024, 256
    xp = x_prompt.reshape(mp, D)
    xs = x_sample.reshape(bs, D)
    chunk_vs = []

    def mem_both(l, pp, ps, q_block):
        mo_p = mem_attention(pp.reshape(bp, L, -1), mkv, mkv, lead_k=(l, 0), lead_v=(l, 1),
                             q_block=q_block, tm=512, name=f"mem{l}_p")
        mo_s = mem_attention(ps.reshape(bs, 1, -1), smk, smv, lead_k=(l,), lead_v=(l,),
                             q_block=q_block, tm=1, name=f"mem{l}_s")
        return mo_p.reshape(mp, mem_width), mo_s.reshape(bs, mem_width)

    for l in range(depth):
        g_in = W["norm_g"][l]
        if l < n_a:
            ps, wb = rms_matmul_cast(xs, g_in, W["w_in_a"], layer=l, tn=tn_cast, name=f"in_a{l}_s")
            pp = rms_matmul(xp, g_in, wb, tm=tm, tn=tn, out_dtype=BF16, name=f"in_a{l}_p")
            sgu_w = (W["sgu_ln_g"][l], W["sgu_ln_b"][l], W["sgu_w"][l], W["sgu_b"][l])
            main_s, vn = sgu_first_row(ps, *sgu_w, width=e_a, name=f"sgu{l}_s")
            chunk_vs.append(vn)
            main_p = sgu_prompt(pp, *sgu_w, width=e_a, rows=2 * CHUNK, name=f"sgu{l}_p")
            q_block = 3 * e_a // mem_width
            w_out, w_layer = W["w_out_a"], l
        else:
            jb = l - n_a
            lam_init = 0.8 - 0.6 * math.exp(-0.3 * l)
            lam_rows = jnp.stack([W["lambda_q1"][jb], W["lambda_k1"][jb], W["lambda_q2"][jb], W["lambda_k2"][jb]])
            ps, wb = rms_matmul_cast(xs, g_in, W["w_in_b"], layer=jb, tn=tn_fused, column_blocked=True,
                                     name=f"in_b{jb}_s")
            pp, main_s = in_proj_with_decode(xp, g_in, wb, ps, k_s, v_s, *past, W["rel_bias"], lam_rows,
                                             W["subln_g"][jb], lam_init=lam_init, tm=tm_fused,
                                             name=f"in_b{jb}_p_dattn{jb}_s")
            main_s = main_s.astype(BF16)
            main_p = diff_attention_prompt(pp.reshape(bp, L, -1), k_att, v_p, W["rel_bias"], lam_rows,
                                           W["subln_g"][jb], lam_init=lam_init, t=512, name=f"dattn{jb}_p")
            main_p = main_p.reshape(mp, b_width)
            q_block = 2 * b_width // mem_width
            w_out, w_layer = W["w_out_b"], jb
        mo_p, mo_s = mem_both(l, pp, ps, q_block)
        xs, wb = out_proj_cast(main_s, mo_s, w_out, xs, layer=w_layer, tn=tn_cast, name=f"out{l}_s")
        if l < depth - 1:
            xp = out_proj(main_p, mo_p, wb, xp, tm=tm, tn=tn, name=f"out{l}_p")
        else:
            y_p = out_proj_norm(main_p, mo_p, wb, xp, W["final_norm_g"], tm=tm, tn=tn, name=f"out{l}_p")
        if l == n_a - 1:
            g_kv = W["kv_norm_g"]
            k_s, wb_k = rms_matmul_cast(xs, g_kv, W["w_kv"], col0=0, n_cols=b_width, tn=tn_cast, name="k_s")
            v_s, wb_v = rms_matmul_cast(xs, g_kv, W["w_kv"], col0=b_width, n_cols=b_width, tn=tn_cast, name="v_s")
            k_p, k_att = k_proj_prompt(xp, g_kv, wb_k, B=bp, tm=tm, heads=4, name="k_p")
            v_p = v_proj_prompt(xp, g_kv, wb_v, B=bp, tm=tm, heads=4, name="v_p")
    y_s = rmsnorm_rows(xs, W["final_norm_g"], tm=bs, name="final_s")
    return (y_p.reshape(bp, L, D), y_s.reshape(bs, 1, D),
            k_p.reshape(bp, L, H, 2, B_HD), jnp.transpose(v_p, (0, 2, 1, 3)),
            k_s.reshape(bs, 1, H, 2, B_HD), v_s.reshape(bs, 1, H, B_HD2), chunk_vs)


def kernel(x_prompt, x_sample, cache_k, cache_v, cache_mem_k, cache_mem_v, page_table, mem_prompt,
           norm_g, final_norm_g, mem_norm_g, w_mem_kv, w_in_a, sgu_ln_g, sgu_ln_b, sgu_w, sgu_b,
           w_out_a, kv_norm_g, w_kv, w_in_b, lambda_q1, lambda_k1, lambda_q2, lambda_k2,
           subln_g, rel_bias, w_out_b):
    depth = norm_g.shape[0]
    b_width = w_kv.shape[1] // 2
    mem_width = w_mem_kv.shape[-1] // 2
    mem_hd = mem_width // MEM_HEADS
    W = dict(norm_g=norm_g, final_norm_g=final_norm_g, sgu_ln_g=sgu_ln_g, sgu_ln_b=sgu_ln_b, sgu_w=sgu_w,
             sgu_b=sgu_b, kv_norm_g=kv_norm_g, lambda_q1=lambda_q1, lambda_k1=lambda_k1, lambda_q2=lambda_q2,
             lambda_k2=lambda_k2, subln_g=subln_g, rel_bias=rel_bias,
             w_in_a=w_in_a, w_out_a=w_out_a, w_in_b=w_in_b, w_out_b=w_out_b, w_kv=w_kv)

    bd, dec_seq = x_sample.shape[:2]
    assert dec_seq == 1, "the sample group is one new token per sequence"
    past = (cache_k.reshape(-1, B_HD), jnp.transpose(cache_v, (0, 2, 1, 3)), page_table)
    smk = cache_mem_k.reshape(depth, bd, -1, mem_width)
    smv = cache_mem_v.reshape(depth, bd, -1, mem_width)

    bp, m_len, d_model = mem_prompt.shape
    mkv = rms_matmul_stacked(mem_prompt.reshape(bp * m_len, d_model), mem_norm_g, w_mem_kv.astype(BF16),
                             tm=512, tn=mem_width, name="mem_kv")
    mkv = mkv.reshape(depth, 2, bp, m_len, mem_width)

    y_p, y_s, k_p, v_p, k_s, v_s, chunk_vs = _trunk(x_prompt, x_sample, mkv, smk, smv, past, W)
    chunk_v_s = jnp.stack(chunk_vs).reshape(len(chunk_vs), bd, dec_seq, -1)
    mem_shape = (depth, bp, m_len, MEM_HEADS, mem_hd)
    return (y_p, y_s, k_p, v_p, k_s, v_s, mkv[:, 0].reshape(mem_shape), mkv[:, 1].reshape(mem_shape), chunk_v_s)
```

```python
import functools
import math

import jax
import jax.numpy as jnp
from jax import lax
from jax.experimental import pallas as pl
from jax.experimental.pallas import tpu as pltpu

F32 = jnp.float32
BF16 = jnp.bfloat16

CHUNK = 128
SGU_GROUPS = 8
B_HD2 = 256
B_HD = B_HD2 // 2
MEM_HEADS = 4
N_BUCKETS = 32
MAX_DISTANCE = 128
PAGE_SIZE = 128
EPS = 1e-6
NEG = -1e30

VMEM_BYTES_V7X = 64 * 1024 * 1024
MXU_DEPTH_V7X = 256
VMEM_LIMIT = VMEM_BYTES_V7X - 8 * 1024 * 1024

NT_DIMS = (((1,), (1,)), ((), ()))


def _params(*sem):
    return pltpu.CompilerParams(dimension_semantics=sem, vmem_limit_bytes=VMEM_LIMIT)


def _normalised_rows(x_ref, g_ref, h_ref):
    @pl.when(pl.program_id(1) == 0)
    def _():
        x = x_ref[...]
        ms = jnp.mean(x * x, axis=-1, keepdims=True)
        h_ref[...] = (x * lax.rsqrt(ms + EPS) * g_ref[...]).astype(BF16)

    return h_ref[...]


def _rms_dot(x_ref, g_ref, w_ref, h_ref, emit):
    first = pl.program_id(1) == 0

    @pl.when(first)
    def _():
        x = x_ref[...]
        r = lax.rsqrt(jnp.mean(x * x, axis=-1, keepdims=True) + EPS)
        kc = min(MXU_DEPTH_V7X, x_ref.shape[1])
        acc = None
        for c in range(x_ref.shape[1] // kc):
            cs = slice(c * kc, (c + 1) * kc)
            hc = (x_ref[:, cs] * r * g_ref[:, cs]).astype(BF16)
            h_ref[:, cs] = hc
            part = jnp.dot(hc, w_ref[cs, :], preferred_element_type=F32)
            acc = part if acc is None else acc + part
        emit(acc)

    @pl.when(jnp.logical_not(first))
    def _():
        emit(jnp.dot(h_ref[...], w_ref[...], preferred_element_type=F32))


def _rms_matmul_kernel(x_ref, g_ref, w_ref, o_ref, h_ref):
    def emit(res):
        o_ref[...] = res

    _rms_dot(x_ref, g_ref, w_ref, h_ref, emit)


def rms_matmul(x, g, w, *, tm, tn, name):
    M, D = x.shape
    N = w.shape[1]
    return pl.pallas_call(
        _rms_matmul_kernel,
        grid=(M // tm, N // tn),
        in_specs=[pl.BlockSpec((tm, D), lambda i, j: (i, 0)),
                  pl.BlockSpec((1, D), lambda i, j: (0, 0)),
                  pl.BlockSpec((D, tn), lambda i, j: (0, j))],
        out_specs=pl.BlockSpec((tm, tn), lambda i, j: (i, j)),
        out_shape=jax.ShapeDtypeStruct((M, N), F32),
        scratch_shapes=[pltpu.VMEM((tm, D), BF16)],
        compiler_params=_params("parallel", "arbitrary"),
        name=name,
    )(x, g.reshape(1, D), w)


def _rms_matmul_cast_kernel(x_ref, g_ref, w_ref, o_ref, wb_ref, h_ref):
    h = _normalised_rows(x_ref, g_ref, h_ref)
    wb = w_ref[...].astype(BF16)
    wb_ref[...] = wb
    o_ref[...] = jnp.dot(h, wb, preferred_element_type=F32)


def _f32_weight_spec(w, layer, tn, first_block=0):
    if w.ndim == 2:
        return pl.BlockSpec((w.shape[0], tn), lambda i, j: (0, first_block + j))
    return pl.BlockSpec((None, w.shape[1], tn), lambda i, j: (layer, 0, first_block + j))


def rms_matmul_cast(x, g, w, *, tn, name, layer=0, col0=0, n_cols=None, column_blocked=False):
    M, D = x.shape
    n_cols = w.shape[-1] if n_cols is None else n_cols
    if column_blocked:
        wb_spec = pl.BlockSpec((None, D, tn), lambda i, j: (j, 0, 0))
        wb_shape = jax.ShapeDtypeStruct((n_cols // tn, D, tn), BF16)
    else:
        wb_spec = pl.BlockSpec((D, tn), lambda i, j: (0, j))
        wb_shape = jax.ShapeDtypeStruct((D, n_cols), BF16)
    return pl.pallas_call(
        _rms_matmul_cast_kernel,
        grid=(1, n_cols // tn),
        in_specs=[pl.BlockSpec((M, D), lambda i, j: (0, 0)),
                  pl.BlockSpec((1, D), lambda i, j: (0, 0)),
                  _f32_weight_spec(w, layer, tn, col0 // tn)],
        out_specs=[pl.BlockSpec((M, tn), lambda i, j: (0, j)), wb_spec],
        out_shape=[jax.ShapeDtypeStruct((M, n_cols), F32), wb_shape],
        scratch_shapes=[pltpu.VMEM((M, D), BF16)],
        compiler_params=_params("arbitrary", "arbitrary"),
        name=name,
    )(x, g.reshape(1, D), w)


def _rms_matmul_f32w_kernel(x_ref, g_ref, w_ref, o_ref, h_ref):
    h = _normalised_rows(x_ref, g_ref, h_ref)
    o_ref[...] = jnp.dot(h, w_ref[...].astype(BF16), preferred_element_type=F32)


def rms_matmul_stacked(x, g, w, *, width, tn, name):
    M, D = x.shape
    G, _, N = w.shape
    nb, nt = N // tn, width // tn
    return pl.pallas_call(
        _rms_matmul_f32w_kernel,
        grid=(1, G * nb),
        in_specs=[pl.BlockSpec((M, D), lambda i, j: (0, 0), pipeline_mode=pl.Buffered(1)),
                  pl.BlockSpec((1, D), lambda i, j: (0, 0)),
                  pl.BlockSpec((None, D, tn), lambda i, j: (j // nb, 0, j % nb))],
        out_specs=pl.BlockSpec((None, M, tn), lambda i, j: (j // nt, 0, j % nt)),
        out_shape=jax.ShapeDtypeStruct((G * N // width, M, width), F32),
        scratch_shapes=[pltpu.VMEM((M, D), BF16)],
        compiler_params=_params("arbitrary", "arbitrary"),
        name=name,
    )(x, g.reshape(1, D), w)


def _k_proj_kernel(x_ref, g_ref, w_ref, knat_ref, katt_ref, h_ref):
    tm = x_ref.shape[0]
    heads = katt_ref.shape[0]
    j = pl.program_id(1)
    n_sub = knat_ref.shape[0] // tm

    def emit(res):
        for q in range(2 * heads):
            knat_ref[pl.ds(j * 2 * heads + q, tm, stride=n_sub), :] = res[:, q * B_HD:(q + 1) * B_HD]
        for hh in range(heads):
            katt_ref[hh] = res[:, hh * B_HD2:(hh + 1) * B_HD2].astype(BF16)

    _rms_dot(x_ref, g_ref, w_ref, h_ref, emit)


def k_proj_prompt(x, g, w, *, B, tm, heads, name):
    M, D = x.shape
    L = M // B
    N = w.shape[1]
    H = N // B_HD2
    nlb = L // tm
    return pl.pallas_call(
        _k_proj_kernel,
        grid=(M // tm, H // heads),
        in_specs=[pl.BlockSpec((tm, D), lambda i, j: (i, 0)),
                  pl.BlockSpec((1, D), lambda i, j: (0, 0)),
                  pl.BlockSpec((D, heads * B_HD2), lambda i, j: (0, j))],
        out_specs=[pl.BlockSpec((tm * 2 * H, B_HD), lambda i, j: (i, 0)),
                   pl.BlockSpec((None, heads, tm, B_HD2), lambda i, j: (i // nlb, j, i % nlb, 0))],
        out_shape=[jax.ShapeDtypeStruct((M * 2 * H, B_HD), F32),
                   jax.ShapeDtypeStruct((B, H, L, B_HD2), BF16)],
        scratch_shapes=[pltpu.VMEM((tm, D), BF16)],
        compiler_params=_params("parallel", "arbitrary"),
        name=name,
    )(x, g.reshape(1, D), w)


def _v_proj_kernel(x_ref, g_ref, w_ref, v_ref, h_ref):
    def emit(res):
        for hh in range(v_ref.shape[0]):
            v_ref[hh] = res[:, hh * B_HD2:(hh + 1) * B_HD2]

    _rms_dot(x_ref, g_ref, w_ref, h_ref, emit)


def v_proj_prompt(x, g, w, *, B, tm, heads, name):
    M, D = x.shape
    L = M // B
    H = w.shape[1] // B_HD2
    nlb = L // tm
    return pl.pallas_call(
        _v_proj_kernel,
        grid=(M // tm, H // heads),
        in_specs=[pl.BlockSpec((tm, D), lambda i, j: (i, 0)),
                  pl.BlockSpec((1, D), lambda i, j: (0, 0)),
                  pl.BlockSpec((D, heads * B_HD2), lambda i, j: (0, j))],
        out_specs=pl.BlockSpec((None, heads, tm, B_HD2), lambda i, j: (i // nlb, j, i % nlb, 0)),
        out_shape=jax.ShapeDtypeStruct((B, H, L, B_HD2), F32),
        scratch_shapes=[pltpu.VMEM((tm, D), BF16)],
        compiler_params=_params("parallel", "arbitrary"),
        name=name,
    )(x, g.reshape(1, D), w)


def _out_proj_kernel(a_ref, m_ref, w_ref, x_ref, o_ref):
    ka = a_ref.shape[1]
    acc = jnp.dot(a_ref[...], w_ref[:ka, :], preferred_element_type=F32)
    acc = acc + jnp.dot(m_ref[...], w_ref[ka:, :], preferred_element_type=F32)
    o_ref[...] = x_ref[...] + acc


def out_proj(a, m, w, x, *, tm, tn, name):
    M, ka = a.shape
    km = m.shape[1]
    N = w.shape[1]
    return pl.pallas_call(
        _out_proj_kernel,
        grid=(M // tm, N // tn),
        in_specs=[pl.BlockSpec((tm, ka), lambda i, j: (i, 0)),
                  pl.BlockSpec((tm, km), lambda i, j: (i, 0)),
                  pl.BlockSpec((ka + km, tn), lambda i, j: (0, j)),
                  pl.BlockSpec((tm, tn), lambda i, j: (i, j))],
        out_specs=pl.BlockSpec((tm, tn), lambda i, j: (i, j)),
        out_shape=jax.ShapeDtypeStruct((M, N), F32),
        compiler_params=_params("parallel", "arbitrary"),
        name=name,
    )(a, m, w, x)


def _out_proj_cast_kernel(a_ref, m_ref, w_ref, x_ref, o_ref, wb_ref):
    ka = a_ref.shape[1]
    wb = w_ref[...].astype(BF16)
    wb_ref[...] = wb
    acc = jnp.dot(a_ref[...], wb[:ka], preferred_element_type=F32)
    acc = acc + jnp.dot(m_ref[...], wb[ka:], preferred_element_type=F32)
    o_ref[...] = x_ref[...] + acc


def out_proj_cast(a, m, w, x, *, tn, name, layer=0):
    M, ka = a.shape
    km = m.shape[1]
    N = w.shape[-1]
    return pl.pallas_call(
        _out_proj_cast_kernel,
        grid=(1, N // tn),
        in_specs=[pl.BlockSpec((M, ka), lambda i, j: (0, 0)),
                  pl.BlockSpec((M, km), lambda i, j: (0, 0)),
                  _f32_weight_spec(w, layer, tn),
                  pl.BlockSpec((M, tn), lambda i, j: (0, j))],
        out_specs=[pl.BlockSpec((M, tn), lambda i, j: (0, j)),
                   pl.BlockSpec((ka + km, tn), lambda i, j: (0, j))],
        out_shape=[jax.ShapeDtypeStruct((M, N), F32),
                   jax.ShapeDtypeStruct((ka + km, N), BF16)],
        compiler_params=_params("arbitrary", "arbitrary"),
        name=name,
    )(a, m, w, x)


def _out_proj_norm_kernel(a_ref, m_ref, w_ref, x_ref, g_ref, o_ref):
    j = pl.program_id(1)
    ka = a_ref.shape[1]
    tn = x_ref.shape[1]
    acc = jnp.dot(a_ref[...], w_ref[:ka, :], preferred_element_type=F32)
    acc = acc + jnp.dot(m_ref[...], w_ref[ka:, :], preferred_element_type=F32)
    o_ref[:, pl.ds(pl.multiple_of(j * tn, tn), tn)] = x_ref[...] + acc

    @pl.when(j == pl.num_programs(1) - 1)
    def _():
        y = o_ref[...]
        ms = jnp.mean(y * y, axis=-1, keepdims=True)
        o_ref[...] = y * lax.rsqrt(ms + EPS) * g_ref[...]


def out_proj_norm(a, m, w, x, g, *, tm, tn, name):
    M, ka = a.shape
    km = m.shape[1]
    N = w.shape[-1]
    return pl.pallas_call(
        _out_proj_norm_kernel,
        grid=(M // tm, N // tn),
        in_specs=[pl.BlockSpec((tm, ka), lambda i, j: (i, 0)),
                  pl.BlockSpec((tm, km), lambda i, j: (i, 0)),
                  pl.BlockSpec((ka + km, tn), lambda i, j: (0, j)),
                  pl.BlockSpec((tm, tn), lambda i, j: (i, j)),
                  pl.BlockSpec((1, N), lambda i, j: (0, 0))],
        out_specs=pl.BlockSpec((tm, N), lambda i, j: (i, 0)),
        out_shape=jax.ShapeDtypeStruct((M, N), F32),
        compiler_params=_params("parallel", "arbitrary"),
        name=name,
    )(a, m, w, x, g.reshape(1, N))


def _rmsnorm_kernel(x_ref, g_ref, o_ref):
    x = x_ref[...]
    ms = jnp.mean(x * x, axis=-1, keepdims=True)
    o_ref[...] = x * lax.rsqrt(ms + EPS) * g_ref[...]


def rmsnorm_rows(x, g, *, tm, name):
    M, D = x.shape
    tm = min(tm, M)
    return pl.pallas_call(
        _rmsnorm_kernel,
        grid=(M // tm,),
        in_specs=[pl.BlockSpec((tm, D), lambda i: (i, 0)),
                  pl.BlockSpec((1, D), lambda i: (0, 0))],
        out_specs=pl.BlockSpec((tm, D), lambda i: (i, 0)),
        out_shape=jax.ShapeDtypeStruct((M, D), F32),
        compiler_params=_params("parallel"),
        name=name,
    )(x, g.reshape(1, D))


def _layernorm(v, g, b):
    mu = jnp.mean(v, axis=-1, keepdims=True)
    vc = v - mu
    var = jnp.mean(vc * vc, axis=-1, keepdims=True)
    return vc * lax.rsqrt(var + EPS) * g + b


def _sgu_kernel(u_ref, v_ref, z_ref, lng_ref, lnb_ref, w_ref, bt_ref, o_ref):
    rows, width = u_ref.shape
    ch = width // SGU_GROUPS
    r = lax.broadcasted_iota(jnp.int32, (CHUNK, CHUNK), 0)
    c = lax.broadcasted_iota(jnp.int32, (CHUNK, CHUNK), 1)
    causal = c <= r
    ws = [jnp.where(causal, w_ref[g], 0.0).astype(BF16) for g in range(SGU_GROUPS)]
    for n in range(rows // CHUNK):
        rs = slice(n * CHUNK, (n + 1) * CHUNK)
        vn = _layernorm(jax.nn.gelu(v_ref[rs, :]), lng_ref[...], lnb_ref[...]).astype(BF16)
        for g in range(SGU_GROUPS):
            cs = slice(g * ch, (g + 1) * ch)
            mixed = jnp.dot(ws[g], vn[:, cs], preferred_element_type=F32) + bt_ref[:, g:g + 1]
            u = jax.nn.gelu(u_ref[rs, cs])
            o_ref[rs, cs] = (u * mixed * jax.nn.silu(z_ref[rs, cs])).astype(o_ref.dtype)


def sgu_prompt(p, ln_g, ln_b, w_s, b_s, *, width, rows, name):
    M = p.shape[0]
    return pl.pallas_call(
        _sgu_kernel,
        grid=(M // rows,),
        in_specs=[pl.BlockSpec((rows, width), lambda i: (i, 0)),
                  pl.BlockSpec((rows, width), lambda i: (i, 1)),
                  pl.BlockSpec((rows, width), lambda i: (i, 2)),
                  pl.BlockSpec((1, width), lambda i: (0, 0)),
                  pl.BlockSpec((1, width), lambda i: (0, 0)),
                  pl.BlockSpec((SGU_GROUPS, CHUNK, CHUNK), lambda i: (0, 0, 0)),
                  pl.BlockSpec((CHUNK, SGU_GROUPS), lambda i: (0, 0))],
        out_specs=pl.BlockSpec((rows, width), lambda i: (i, 0)),
        out_shape=jax.ShapeDtypeStruct((M, width), BF16),
        compiler_params=_params("parallel"),
        name=name,
    )(p, p, p, ln_g.reshape(1, width), ln_b.reshape(1, width), w_s, b_s.T)


def _sgu_first_row_kernel(u_ref, v_ref, z_ref, lng_ref, lnb_ref, w0_ref, b0_ref, o_ref, vn_ref):
    vn = _layernorm(jax.nn.gelu(v_ref[...]), lng_ref[...], lnb_ref[...])
    vn_ref[...] = vn
    mixed = w0_ref[...] * vn + b0_ref[...]
    o_ref[...] = (jax.nn.gelu(u_ref[...]) * mixed * jax.nn.silu(z_ref[...])).astype(o_ref.dtype)


def sgu_first_row(p, ln_g, ln_b, w_s, b_s, *, width, name):
    M = p.shape[0]
    ch = width // SGU_GROUPS
    w0 = jnp.repeat(w_s[:, 0, 0], ch).reshape(1, width)
    b0 = jnp.repeat(b_s[:, 0], ch).reshape(1, width)
    row = pl.BlockSpec((1, width), lambda i: (0, 0))
    return pl.pallas_call(
        _sgu_first_row_kernel,
        grid=(1,),
        in_specs=[pl.BlockSpec((M, width), lambda i: (0, 0)),
                  pl.BlockSpec((M, width), lambda i: (0, 1)),
                  pl.BlockSpec((M, width), lambda i: (0, 2)),
                  row, row, row, row],
        out_specs=[pl.BlockSpec((M, width), lambda i: (0, 0)),
                   pl.BlockSpec((M, width), lambda i: (0, 0))],
        out_shape=[jax.ShapeDtypeStruct((M, width), BF16),
                   jax.ShapeDtypeStruct((M, width), F32)],
        compiler_params=_params("arbitrary"),
        name=name,
    )(p, p, p, ln_g.reshape(1, width), ln_b.reshape(1, width), w0, b0)


def _mem_attn_kernel(q_ref, zm_ref, k_ref, v_ref, o_ref):
    tm, width = q_ref.shape
    hd = width // MEM_HEADS
    q = q_ref[...]
    if tm < 8:
        q = jnp.broadcast_to(q[0:1], (8, width))
    for h in range(MEM_HEADS):
        cs = slice(h * hd, (h + 1) * hd)
        s = lax.dot_general(q[:, cs].astype(BF16), k_ref[:, cs].astype(BF16), NT_DIMS,
                            preferred_element_type=F32) * (hd ** -0.5)
        e = jnp.exp(s - jnp.max(s, axis=-1, keepdims=True))
        p = e * (1.0 / jnp.sum(e, axis=-1, keepdims=True))
        o = jnp.dot(p.astype(BF16), v_ref[:, cs].astype(BF16), preferred_element_type=F32)
        o_ref[:, cs] = (o[:tm] * jax.nn.silu(zm_ref[:, cs])).astype(o_ref.dtype)


def mem_attention(p, mem_k, mem_v, *, lead_k, lead_v, q_block, tm, name):
    B, L, _ = p.shape
    mlen, width = mem_k.shape[-2:]
    tm = min(tm, L)
    mem_spec = lambda lead: pl.BlockSpec((None,) * (len(lead) + 1) + (mlen, width),
                                         lambda b, i: (*lead, b, 0, 0))
    return pl.pallas_call(
        _mem_attn_kernel,
        grid=(B, L // tm),
        in_specs=[pl.BlockSpec((None, tm, width), lambda b, i: (b, i, q_block)),
                  pl.BlockSpec((None, tm, width), lambda b, i: (b, i, q_block + 1)),
                  mem_spec(lead_k), mem_spec(lead_v)],
        out_specs=pl.BlockSpec((None, tm, width), lambda b, i: (b, i, 0)),
        out_shape=jax.ShapeDtypeStruct((B, L, width), BF16),
        compiler_params=_params("parallel", "parallel"),
        name=name,
    )(p, p, mem_k, mem_v)


def _rel_bucket(dist):
    n = jnp.maximum(dist, 0)
    max_exact = N_BUCKETS // 2
    nf = jnp.maximum(n, 1).astype(F32)
    large = max_exact + (jnp.log(nf / max_exact) / math.log(MAX_DISTANCE / max_exact)
                         * (N_BUCKETS - max_exact)).astype(jnp.int32)
    large = jnp.minimum(large, N_BUCKETS - 1)
    return jnp.where(n < max_exact, n, large)


def _lambda(lam_ref, lam_init):
    a = jnp.sum(lam_ref[0:1, :] * lam_ref[1:2, :], axis=-1, keepdims=True)
    b = jnp.sum(lam_ref[2:3, :] * lam_ref[3:4, :], axis=-1, keepdims=True)
    return jnp.exp(a) - jnp.exp(b) + lam_init


def _subln_gate(o, g, z, lam_init):
    ms = jnp.mean(o * o, axis=-1, keepdims=True)
    on = (o * lax.rsqrt(ms + EPS) * g) * (1.0 - lam_init)
    return on * jax.nn.silu(z)


def _lane_groups(x):
    return [x[:, g * 128:(g + 1) * 128] for g in range(x.shape[1] // 128)]


def _diff_attn_kernel(q_ref, z_ref, k_ref, v_ref, bd_ref, be_ref, bfar_ref, lam_ref, g_ref, o_ref,
                      vb_ref, b0_ref, b1_ref, s_ref, mp_ref, lp_ref, acc_ref, *, lam_init):
    t = q_ref.shape[0]
    qi = pl.program_id(2)
    scale = B_HD ** -0.5
    md = MAX_DISTANCE
    nb = t // md

    @pl.when(qi == 0)
    def _():
        vb_ref[...] = v_ref[...].astype(BF16)
        far = jnp.broadcast_to(bfar_ref[0:1, 0:1], (md, md))
        masked = jnp.full((md, md), NEG, F32)
        near, next_to_near = bd_ref[...], be_ref[...]
        for bi in range(nb):
            for bj in range(nb):
                piece = near if bi == bj else next_to_near if bi == bj + 1 else far if bi > bj else masked
                b0_ref[bi * md:(bi + 1) * md, bj * md:(bj + 1) * md] = piece
                piece = next_to_near if (bi == 0 and bj == nb - 1) else far
                b1_ref[bi * md:(bi + 1) * md, bj * md:(bj + 1) * md] = piece

    q = q_ref[...]
    qs = (q[:, :B_HD].astype(BF16), q[:, B_HD:].astype(BF16))
    mp_ref[...] = jnp.full(mp_ref.shape, NEG, F32)
    lp_ref[...] = jnp.zeros(lp_ref.shape, F32)
    acc_ref[...] = jnp.zeros(acc_ref.shape, F32)

    def scores(kj, bias):
        off = pl.multiple_of(kj * t, t)
        k = k_ref[pl.ds(off, t), :]
        for c in range(2):
            s = lax.dot_general(qs[c], k[:, c * B_HD:(c + 1) * B_HD], NT_DIMS,
                                preferred_element_type=F32) * scale + bias
            s_ref[c, :, pl.ds(off, t)] = s
            mp_ref[c] = functools.reduce(jnp.maximum, _lane_groups(s), mp_ref[c])

    bfar = bfar_ref[0:1, 0:1]

    def far(kj, carry):
        scores(kj, bfar)
        return carry

    lax.fori_loop(0, jnp.maximum(qi - 1, 0), far, 0)

    @pl.when(qi >= 1)
    def _():
        scores(qi - 1, b1_ref[...])

    scores(qi, b0_ref[...])

    m_rows = [jnp.broadcast_to(jnp.max(mp_ref[c], axis=-1, keepdims=True), (t, 128)) for c in range(2)]

    def accumulate(kj, carry):
        off = pl.multiple_of(kj * t, t)
        v = vb_ref[pl.ds(off, t), :]
        for c in range(2):
            ps = [jnp.exp(sg - m_rows[c]) for sg in _lane_groups(s_ref[c, :, pl.ds(off, t)])]
            lp_ref[c] = functools.reduce(jnp.add, ps, lp_ref[c])
            p = jnp.concatenate(ps, axis=1).astype(BF16)
            acc_ref[c] = acc_ref[c] + jnp.dot(p, v, preferred_element_type=F32)
        return carry

    lax.fori_loop(0, qi + 1, accumulate, 0)

    lam = _lambda(lam_ref, lam_init)
    inv_l = [1.0 / jnp.sum(lp_ref[c], axis=-1, keepdims=True) for c in range(2)]
    o = acc_ref[0] * inv_l[0] - lam * (acc_ref[1] * inv_l[1])
    o_ref[...] = _subln_gate(o, g_ref[...], z_ref[...], lam_init).astype(o_ref.dtype)


def _toeplitz_tiles(g, t):
    H, P = g.shape
    hankel = jnp.tile(g[:, ::-1], (1, t + 1))[:, :t * (P + 1)].reshape(H, t, P + 1)
    return hankel[:, ::-1, :t]


def diff_attention_prompt(p, k, v, rel_bias, lam_rows, subln_g, *, lam_init, t, name):
    B, L, _ = p.shape
    H = k.shape[1]
    md = MAX_DISTANCE
    assert t % md == 0 and L % t == 0
    table = rel_bias.astype(F32)
    d = jnp.arange(-(md - 1), md, dtype=jnp.int32)
    bd = _toeplitz_tiles(jnp.where((d >= 0)[None], table[_rel_bucket(d)].T, NEG), md)
    be = _toeplitz_tiles(table[_rel_bucket(d + md)].T, md)
    bfar = jnp.broadcast_to(table[_rel_bucket(jnp.int32(md + 1))][:, None, None], (H, 1, 128))
    blk = lambda off: pl.BlockSpec((None, t, B_HD2), lambda b, h, i: (b, i, off + h))
    kv = pl.BlockSpec((None, None, L, B_HD2), lambda b, h, i: (b, h, 0, 0))
    tile = pl.BlockSpec((None, md, md), lambda b, h, i: (h, 0, 0))
    return pl.pallas_call(
        functools.partial(_diff_attn_kernel, lam_init=lam_init),
        grid=(B, H, L // t),
        in_specs=[blk(0), blk(H), kv, kv, tile, tile,
                  pl.BlockSpec((None, 1, 128), lambda b, h, i: (h, 0, 0)),
                  pl.BlockSpec((4, B_HD), lambda b, h, i: (0, 0)),
                  pl.BlockSpec((1, B_HD2), lambda b, h, i: (0, 0))],
        out_specs=pl.BlockSpec((None, t, B_HD2), lambda b, h, i: (b, i, h)),
        out_shape=jax.ShapeDtypeStruct((B, L, H * B_HD2), BF16),
        scratch_shapes=[pltpu.VMEM((L, B_HD2), BF16), pltpu.VMEM((t, t), F32), pltpu.VMEM((t, t), F32),
                        pltpu.VMEM((2, t, L), F32), pltpu.VMEM((2, t, 128), F32), pltpu.VMEM((2, t, 128), F32),
                        pltpu.VMEM((2, t, B_HD2), F32)],
        compiler_params=_params("parallel", "parallel", "arbitrary"),
        name=name,
    )(p, p, k, v, bd, be, bfar, lam_rows, subln_g.reshape(1, B_HD2))


def _in_proj_decode_kernel(pt_ref, x_ref, g_ref, w_ref, ps_ref, kn_ref, vn_ref, sb_ref, lam_ref, sg_ref, *rest,
                           pps, lam_init, steps_per_seq):
    del pt_ref
    k_pages, v_pages = rest[:pps], rest[pps:2 * pps]
    o_ref, d_ref, h_ref, qb_ref, m_ref, l_ref, acc_ref = rest[2 * pps:]
    H, rows, _ = qb_ref.shape
    width = H * B_HD2
    step = pl.program_id(0) * pl.num_programs(1) + pl.program_id(1)
    b = step // steps_per_seq
    first = step % steps_per_seq == 0
    last = step % steps_per_seq == steps_per_seq - 1
    scale = B_HD ** -0.5

    def query_rows(hd):
        r_id = lax.broadcasted_iota(jnp.int32, (rows, B_HD2), 0)
        c_id = lax.broadcasted_iota(jnp.int32, (rows, B_HD2), 1)
        own = (c_id >> 7) == r_id
        return jnp.where(own, ps_ref[pl.ds(b, 1), hd * B_HD2:(hd + 1) * B_HD2], 0.0)

    @pl.when(first)
    def _():
        for hd in range(H):
            qb_ref[hd] = query_rows(hd).astype(BF16)
        m_ref[...] = jnp.full(m_ref.shape, NEG, F32)
        l_ref[...] = jnp.zeros(l_ref.shape, F32)
        acc_ref[...] = jnp.zeros(acc_ref.shape, F32)

    def softmax_step(hd, s):
        m_old = m_ref[hd]
        m_new = jnp.maximum(m_old, jnp.max(s, axis=-1, keepdims=True))
        alpha = jnp.exp(m_old - m_new)
        p = jnp.exp(s - m_new)
        l_ref[hd] = alpha * l_ref[hd] + jnp.sum(p, axis=-1, keepdims=True)
        m_ref[hd] = m_new
        return alpha, p

    def head_scores(hd):
        kh = jnp.concatenate(
            [jnp.concatenate([kp[pl.ds(2 * hd + c, PAGE_SIZE, stride=2 * H), :] for c in range(2)], axis=1)
             for kp in k_pages], axis=0).astype(BF16)
        bias = jnp.concatenate([sb_ref[0, hd:hd + 1, :]] * (pps - 1)
                               + [jnp.where(last, sb_ref[1, hd:hd + 1, :], sb_ref[0, hd:hd + 1, :])], axis=1)
        return lax.dot_general(qb_ref[hd], kh, NT_DIMS, preferred_element_type=F32) * scale + bias

    def head_values(hd, alpha, p):
        vh = jnp.concatenate([vp[hd] for vp in v_pages], axis=0).astype(BF16)
        acc_ref[hd] = alpha * acc_ref[hd] + jnp.dot(p.astype(BF16), vh, preferred_element_type=F32)

    kc = min(MXU_DEPTH_V7X, x_ref.shape[1])
    n_chunks = x_ref.shape[1] // kc
    lag = 4

    def tile_and_pages(normalise):
        if normalise:
            x = x_ref[...]
            r = lax.rsqrt(jnp.mean(x * x, axis=-1, keepdims=True) + EPS)
        folded = {}
        acc = None
        for c in range(n_chunks):
            cs = slice(c * kc, (c + 1) * kc)
            if normalise:
                hc = (x_ref[:, cs] * r * g_ref[:, cs]).astype(BF16)
                h_ref[:, cs] = hc
            else:
                hc = h_ref[:, cs]
            part = jnp.dot(hc, w_ref[cs, :], preferred_element_type=F32)
            acc = part if acc is None else acc + part
            for hd in range(H):
                if hd * n_chunks // (H + lag) == c:
                    folded[hd] = softmax_step(hd, head_scores(hd))
            for hd in range(H):
                if (hd + lag) * n_chunks // (H + lag) == c:
                    head_values(hd, *folded[hd])
        o_ref[...] = acc

    first_col = pl.program_id(1) == 0
    pl.when(first_col)(lambda: tile_and_pages(True))
    pl.when(jnp.logical_not(first_col))(lambda: tile_and_pages(False))

    @pl.when(last)
    def _():
        lam = _lambda(lam_ref, lam_init)
        for hd in range(H):
            cs = slice(hd * B_HD2, (hd + 1) * B_HD2)
            s = (jnp.sum(query_rows(hd) * kn_ref[pl.ds(b, 1), cs], axis=-1, keepdims=True) * scale
                 + sb_ref[2, hd:hd + 1, 0:1])
            alpha, p = softmax_step(hd, s)
            o_both = (alpha * acc_ref[hd] + p * vn_ref[pl.ds(b, 1), cs]) * (1.0 / l_ref[hd])
            o = o_both[0:1] - lam * o_both[1:2]
            z = ps_ref[pl.ds(b, 1), width + hd * B_HD2:width + (hd + 1) * B_HD2]
            d_ref[pl.ds(b, 1), cs] = _subln_gate(o, sg_ref[...], z, lam_init)


def in_proj_with_decode(x, g, w, p_s, k_new, v_new, cache_k, cache_v, page_table, rel_bias, lam_rows, subln_g,
                        *, lam_init, tm, name):
    M, D = x.shape
    nj, _, tn = w.shape
    N = nj * tn
    B, n_pages = page_table.shape
    width = k_new.shape[-1]
    H = width // B_HD2
    k_rows = PAGE_SIZE * 2 * H
    ni = M // tm
    assert (B * n_pages) % (ni * nj) == 0, "every grid step takes the same number of pages"
    pps = B * n_pages // (ni * nj)
    assert n_pages % pps == 0 and PAGE_SIZE >= MAX_DISTANCE
    rows = 8
    table = rel_bias.astype(F32)
    dist_last = PAGE_SIZE - jnp.arange(PAGE_SIZE, dtype=jnp.int32)
    sb = jnp.stack([
        jnp.broadcast_to(table[_rel_bucket(jnp.int32(PAGE_SIZE + 1))][:, None], (H, PAGE_SIZE)),
        table[_rel_bucket(dist_last)].T,
        jnp.broadcast_to(table[_rel_bucket(jnp.int32(0))][:, None], (H, PAGE_SIZE)),
    ])
    sg = subln_g.reshape(1, B_HD2)
    whole = lambda a: pl.BlockSpec(a.shape, lambda i, j, pt: (0,) * a.ndim)
    k_page = lambda r: pl.BlockSpec((k_rows, B_HD), lambda i, j, pt: (pt[(i * nj + j) * pps + r], 0))
    v_page = lambda r: pl.BlockSpec((None, H, PAGE_SIZE, B_HD2),
                                    lambda i, j, pt: (pt[(i * nj + j) * pps + r], 0, 0, 0))
    grid_spec = pltpu.PrefetchScalarGridSpec(
        num_scalar_prefetch=1,
        grid=(ni, nj),
        in_specs=[pl.BlockSpec((tm, D), lambda i, j, pt: (i, 0), pipeline_mode=pl.Buffered(1)),
                  pl.BlockSpec((1, D), lambda i, j, pt: (0, 0)),
                  pl.BlockSpec((None, D, tn), lambda i, j, pt: (j, 0, 0)),
                  whole(p_s), whole(k_new), whole(v_new), whole(sb), whole(lam_rows), whole(sg)]
                 + [k_page(r) for r in range(pps)] + [v_page(r) for r in range(pps)],
        out_specs=[pl.BlockSpec((tm, tn), lambda i, j, pt: (i, j)),
                   pl.BlockSpec((B, width), lambda i, j, pt: (0, 0))],
        scratch_shapes=[pltpu.VMEM((tm, D), BF16), pltpu.VMEM((H, rows, B_HD2), BF16),
                        pltpu.VMEM((H, rows, 1), F32), pltpu.VMEM((H, rows, 1), F32),
                        pltpu.VMEM((H, rows, B_HD2), F32)],
    )
    return pl.pallas_call(
        functools.partial(_in_proj_decode_kernel, pps=pps, lam_init=lam_init, steps_per_seq=n_pages // pps),
        grid_spec=grid_spec,
        out_shape=[jax.ShapeDtypeStruct((M, N), F32), jax.ShapeDtypeStruct((B, width), F32)],
        compiler_params=_params("arbitrary", "arbitrary"),
        name=name,
    )(page_table.reshape(-1), x, g.reshape(1, D), w, p_s, k_new, v_new, sb, lam_rows, sg,
      *([cache_k] * pps), *([cache_v] * pps))


def _trunk(x_prompt, x_sample, mkv, smk, smv, past, W):
    bp, L, D = x_prompt.shape
    bs = x_sample.shape[0]
    mp = bp * L
    depth = W["norm_g"].shape[0]
    n_a = W["w_in_a"].shape[0]
    e_a = W["sgu_ln_g"].shape[1]
    b_width = W["w_kv"].shape[1] // 2
    mem_width = mkv.shape[-1]
    H = b_width // B_HD2
    tm, tn = 512, 1024
    tn_cast = 512
    tm_fused, tn_fused = 1024, 256
    xp = x_prompt.reshape(mp, D)
    xs = x_sample.reshape(bs, D)
    chunk_vs = []

    def mem_both(l, pp, ps, q_block):
        mo_p = mem_attention(pp.reshape(bp, L, -1), mkv, mkv, lead_k=(l, 0), lead_v=(l, 1),
                             q_block=q_block, tm=512, name=f"mem{l}_p")
        mo_s = mem_attention(ps.reshape(bs, 1, -1), smk, smv, lead_k=(l,), lead_v=(l,),
                             q_block=q_block, tm=1, name=f"mem{l}_s")
        return mo_p.reshape(mp, mem_width), mo_s.reshape(bs, mem_width)

    for l in range(depth):
        g_in = W["norm_g"][l]
        if l < n_a:
            ps, wb = rms_matmul_cast(xs, g_in, W["w_in_a"], layer=l, tn=tn_cast, name=f"in_a{l}_s")
            pp = rms_matmul(xp, g_in, wb, tm=tm, tn=tn, name=f"in_a{l}_p")
            sgu_w = (W["sgu_ln_g"][l], W["sgu_ln_b"][l], W["sgu_w"][l], W["sgu_b"][l])
            main_s, vn = sgu_first_row(ps, *sgu_w, width=e_a, name=f"sgu{l}_s")
            chunk_vs.append(vn)
            main_p = sgu_prompt(pp, *sgu_w, width=e_a, rows=2 * CHUNK, name=f"sgu{l}_p")
            q_block = 3 * e_a // mem_width
            w_out, w_layer = W["w_out_a"], l
        else:
            jb = l - n_a
            lam_init = 0.8 - 0.6 * math.exp(-0.3 * l)
            lam_rows = jnp.stack([W["lambda_q1"][jb], W["lambda_k1"][jb], W["lambda_q2"][jb], W["lambda_k2"][jb]])
            ps, wb = rms_matmul_cast(xs, g_in, W["w_in_b"], layer=jb, tn=tn_fused, column_blocked=True,
                                     name=f"in_b{jb}_s")
            pp, main_s = in_proj_with_decode(xp, g_in, wb, ps, k_s, v_s, *past, W["rel_bias"], lam_rows,
                                             W["subln_g"][jb], lam_init=lam_init, tm=tm_fused,
                                             name=f"in_b{jb}_p_dattn{jb}_s")
            main_s = main_s.astype(BF16)
            main_p = diff_attention_prompt(pp.reshape(bp, L, -1), k_att, v_p, W["rel_bias"], lam_rows,
                                           W["subln_g"][jb], lam_init=lam_init, t=512, name=f"dattn{jb}_p")
            main_p = main_p.reshape(mp, b_width)
            q_block = 2 * b_width // mem_width
            w_out, w_layer = W["w_out_b"], jb
        mo_p, mo_s = mem_both(l, pp, ps, q_block)
        xs, wb = out_proj_cast(main_s, mo_s, w_out, xs, layer=w_layer, tn=tn_cast, name=f"out{l}_s")
        if l < depth - 1:
            xp = out_proj(main_p, mo_p, wb, xp, tm=tm, tn=tn, name=f"out{l}_p")
        else:
            y_p = out_proj_norm(main_p, mo_p, wb, xp, W["final_norm_g"], tm=tm, tn=tn, name=f"out{l}_p")
        if l == n_a - 1:
            g_kv = W["kv_norm_g"]
            k_s, wb_k = rms_matmul_cast(xs, g_kv, W["w_kv"], col0=0, n_cols=b_width, tn=tn_cast, name="k_s")
            v_s, wb_v = rms_matmul_cast(xs, g_kv, W["w_kv"], col0=b_width, n_cols=b_width, tn=tn_cast, name="v_s")
            k_p, k_att = k_proj_prompt(xp, g_kv, wb_k, B=bp, tm=tm, heads=2, name="k_p")
            v_p = v_proj_prompt(xp, g_kv, wb_v, B=bp, tm=tm, heads=4, name="v_p")
    y_s = rmsnorm_rows(xs, W["final_norm_g"], tm=bs, name="final_s")
    return (y_p.reshape(bp, L, D), y_s.reshape(bs, 1, D),
            k_p.reshape(bp, L, H, 2, B_HD), jnp.transpose(v_p, (0, 2, 1, 3)),
            k_s.reshape(bs, 1, H, 2, B_HD), v_s.reshape(bs, 1, H, B_HD2), chunk_vs)


def kernel(x_prompt, x_sample, cache_k, cache_v, cache_mem_k, cache_mem_v, page_table, mem_prompt,
           norm_g, final_norm_g, mem_norm_g, w_mem_kv, w_in_a, sgu_ln_g, sgu_ln_b, sgu_w, sgu_b,
           w_out_a, kv_norm_g, w_kv, w_in_b, lambda_q1, lambda_k1, lambda_q2, lambda_k2,
           subln_g, rel_bias, w_out_b):
    depth = norm_g.shape[0]
    b_width = w_kv.shape[1] // 2
    mem_width = w_mem_kv.shape[-1] // 2
    mem_hd = mem_width // MEM_HEADS
    W = dict(norm_g=norm_g, final_norm_g=final_norm_g, sgu_ln_g=sgu_ln_g, sgu_ln_b=sgu_ln_b, sgu_w=sgu_w,
             sgu_b=sgu_b, kv_norm_g=kv_norm_g, lambda_q1=lambda_q1, lambda_k1=lambda_k1, lambda_q2=lambda_q2,
             lambda_k2=lambda_k2, subln_g=subln_g, rel_bias=rel_bias,
             w_in_a=w_in_a, w_out_a=w_out_a, w_in_b=w_in_b, w_out_b=w_out_b, w_kv=w_kv)

    bd, dec_seq = x_sample.shape[:2]
    assert dec_seq == 1, "the sample group is one new token per sequence"
    past = (cache_k.reshape(-1, B_HD), jnp.transpose(cache_v, (0, 2, 1, 3)), page_table)
    smk = cache_mem_k.reshape(depth, bd, -1, mem_width)
    smv = cache_mem_v.reshape(depth, bd, -1, mem_width)

    bp, m_len, d_model = mem_prompt.shape
    mkv = rms_matmul_stacked(mem_prompt.reshape(bp * m_len, d_model), mem_norm_g, w_mem_kv,
                             width=mem_width, tn=256, name="mem_kv")
    mkv = mkv.reshape(depth, 2, bp, m_len, mem_width)

    y_p, y_s, k_p, v_p, k_s, v_s, chunk_vs = _trunk(x_prompt, x_sample, mkv, smk, smv, past, W)
    chunk_v_s = jnp.stack(chunk_vs).reshape(len(chunk_vs), bd, dec_seq, -1)
    mem_shape = (depth, bp, m_len, MEM_HEADS, mem_hd)
    return (y_p, y_s, k_p, v_p, k_s, v_s, mkv[:, 0].reshape(mem_shape), mkv[:, 1].reshape(mem_shape), chunk_v_s)
```
